```python
import math
import jax, jax.numpy as jnp
from jax import lax
import numpy as np

D_MODEL = 1024
BATCH = 2
SEQ = 8192
DEPTH = 4
DEC_BATCH = 32
DEC_SEQ = 4
PAST_LEN = 8192
PAGE_SIZE = 128

N_MIXERS = 2
EXPAND = 2
BRANCH_W = EXPAND * D_MODEL
CONV_WIDTH = 31
HEAD_DIM = 64
VAL_DIM = 2 * HEAD_DIM
N_HEADS = BRANCH_W // VAL_DIM
N_KV_HEADS = N_HEADS // 4
GROUP = N_HEADS // N_KV_HEADS
Q_W = N_HEADS * 2 * HEAD_DIM
KV_W = N_KV_HEADS * 2 * HEAD_DIM
ROT_DIM = HEAD_DIM // 4
ROPE_THETA = 500000.0
BLOCK_Q = 128
RMS_EPS = 1e-6
LN_EPS = 1e-5
SCALE = HEAD_DIM ** -0.5
N_CONV_LAYERS = (DEPTH + N_MIXERS - 1) // N_MIXERS
N_ATTN_LAYERS = DEPTH // N_MIXERS

kernel_name = 'hybrid_conformer_conv_diff_attn_step'


def rmsnorm(x, g, eps=RMS_EPS):
    xf = x.astype(jnp.float32)
    y = xf * lax.rsqrt(jnp.mean(xf * xf, axis=-1, keepdims=True) + eps)
    return (y * g.astype(jnp.float32)).astype(x.dtype)


def layernorm(x, g, b, eps=LN_EPS):
    xf = x.astype(jnp.float32)
    mu = jnp.mean(xf, axis=-1, keepdims=True)
    xc = xf - mu
    y = xc * lax.rsqrt(jnp.mean(xc * xc, axis=-1, keepdims=True) + eps)
    return (y * g.astype(jnp.float32) + b.astype(jnp.float32)).astype(x.dtype)


def conv_mixer(h, prev, w_in, dw_w, dw_b, ln_g, ln_b, w_out):
    z = jnp.einsum('ntd,de->nte', h, w_in)
    a, b, gate = jnp.split(z, 3, axis=-1)
    u = a * jax.nn.sigmoid(b)
    u_pad = jnp.concatenate([prev.astype(u.dtype), u], axis=1)
    c = lax.conv_general_dilated(u_pad, dw_w[:, None, :].astype(u.dtype), (1,), 'VALID',
                                 dimension_numbers=('NWC', 'WIO', 'NWC'),
                                 feature_group_count=BRANCH_W) + dw_b
    c = jax.nn.silu(layernorm(c, ln_g, ln_b))
    y = jnp.einsum('nte,ed->ntd', c * jax.nn.silu(gate), w_out)
    return y, u_pad[:, -(CONV_WIDTH - 1):]


def rope(x, pos):
    half = ROT_DIM // 2
    inv_freq = ROPE_THETA ** (-jnp.arange(half, dtype=jnp.float32) / half)
    ang = pos.astype(jnp.float32)[:, None] * inv_freq[None, :]
    cos = jnp.cos(ang)[:, None, None, :]
    sin = jnp.sin(ang)[:, None, None, :]
    xr = x[..., :ROT_DIM].astype(jnp.float32)
    x1, x2 = xr[..., :half], xr[..., half:]
    rot = jnp.concatenate([x1 * cos - x2 * sin, x2 * cos + x1 * sin], axis=-1).astype(x.dtype)
    return jnp.concatenate([rot, x[..., ROT_DIM:]], axis=-1)


def diff_lambda(lq1, lk1, lq2, lk2, lam_init):
    f = lambda a, b: jnp.exp(jnp.sum(a.astype(jnp.float32) * b.astype(jnp.float32)))
    return f(lq1, lk1) - f(lq2, lk2) + lam_init


def attn_project(h, pos, w_in):
    n, t = h.shape[:2]
    z = jnp.einsum('ntd,de->nte', h, w_in)
    q, k, v, gate = jnp.split(z, [Q_W, Q_W + KV_W, Q_W + 2 * KV_W], axis=-1)
    q = rope(q.reshape(n, t, N_HEADS, 2, HEAD_DIM), pos).reshape(n, t, N_KV_HEADS, GROUP, 2, HEAD_DIM)
    k = rope(k.reshape(n, t, N_KV_HEADS, 2, HEAD_DIM), pos)
    v = v.reshape(n, t, N_KV_HEADS, VAL_DIM)
    return q, k, v, gate


def diff_attend(q, k, v, q_pos, k_pos, lam):
    s = jnp.einsum('ntkgch,nskch->nkgcts', q, k, preferred_element_type=jnp.float32) * SCALE
    mask = k_pos[None, :] <= q_pos[:, None]
    s = jnp.where(mask, s, -jnp.inf)
    p = jax.nn.softmax(s, axis=-1)
    a = p[:, :, :, 0] - lam * p[:, :, :, 1]
    return jnp.einsum('nkgts,nskv->ntkgv', a, v.astype(jnp.float32))


def attn_output(o, gate, subln_g, lam_init, w_out):
    n, t = gate.shape[:2]
    o = rmsnorm(o, subln_g) * (1.0 - lam_init)
    o = o.reshape(n, t, BRANCH_W).astype(gate.dtype)
    return jnp.einsum('nte,ed->ntd', o * jax.nn.silu(gate), w_out)


def attn_prompt(h, w_in, lam, subln_g, lam_init, w_out):
    n, t = h.shape[:2]
    pos = jnp.arange(t, dtype=jnp.int32)
    q, k, v, gate = attn_project(h, pos, w_in)
    nb = t // BLOCK_Q
    qb = jnp.moveaxis(q.reshape((n, nb, BLOCK_Q) + q.shape[2:]), 1, 0)
    pb = pos.reshape(nb, BLOCK_Q)
    o = lax.map(lambda qp: diff_attend(qp[0], k, v, qp[1], pos, lam), (qb, pb))
    o = jnp.moveaxis(o, 0, 1).reshape((n, t) + o.shape[3:])
    y = attn_output(o, gate, subln_g, lam_init, w_out)
    return y, k.reshape(n, t, N_KV_HEADS, 2 * HEAD_DIM), v


def attn_sample(h, k_pages, v_pages, page_table, w_in, lam, subln_g, lam_init, w_out):
    n, t = h.shape[:2]
    past = page_table.shape[1] * PAGE_SIZE
    pos = past + jnp.arange(t, dtype=jnp.int32)
    q, k, v, gate = attn_project(h, pos, w_in)
    k_past = k_pages[page_table].reshape(n, past, N_KV_HEADS, 2, HEAD_DIM).astype(k.dtype)
    v_past = v_pages[page_table].reshape(n, past, N_KV_HEADS, VAL_DIM).astype(v.dtype)
    k_all = jnp.concatenate([k_past, k], axis=1)
    v_all = jnp.concatenate([v_past, v], axis=1)
    k_pos = jnp.arange(past + t, dtype=jnp.int32)
    o = diff_attend(q, k_all, v_all, pos, k_pos, lam)
    y = attn_output(o, gate, subln_g, lam_init, w_out)
    return y, k.reshape(n, t, N_KV_HEADS, 2 * HEAD_DIM), v


def setup_inputs(seed: int = 0) -> dict:
    key = jax.random.key(seed)
    ks = jax.random.split(key, 24)
    f32 = jnp.float32
    nrm = lambda k, shape, s: s * jax.random.normal(k, shape, f32)
    n_pages = PAST_LEN // PAGE_SIZE
    n_phys = (DEC_BATCH * n_pages * 5) // 4
    page_table = jax.random.permutation(ks[5], n_phys)[:DEC_BATCH * n_pages]
    page_table = page_table.reshape(DEC_BATCH, n_pages).astype(jnp.int32)
    return {
        'x_prompt': nrm(ks[0], (BATCH, SEQ, D_MODEL), 1.0),
        'x_sample': nrm(ks[1], (DEC_BATCH, DEC_SEQ, D_MODEL), 1.0),
        'state_conv': nrm(ks[2], (N_CONV_LAYERS, DEC_BATCH, CONV_WIDTH - 1, BRANCH_W), 0.5),
        'cache_k': nrm(ks[3], (N_ATTN_LAYERS, n_phys, PAGE_SIZE, N_KV_HEADS, 2 * HEAD_DIM), 1.0),
        'cache_v': nrm(ks[4], (N_ATTN_LAYERS, n_phys, PAGE_SIZE, N_KV_HEADS, VAL_DIM), 1.0),
        'page_table': page_table,
        'norm_g': 1.0 + nrm(ks[6], (DEPTH, D_MODEL), 0.02),
        'final_norm_g': 1.0 + nrm(ks[7], (D_MODEL,), 0.02),
        'conv_w_in': nrm(ks[8], (N_CONV_LAYERS, D_MODEL, 3 * BRANCH_W), D_MODEL ** -0.5),
        'conv_dw_w': nrm(ks[9], (N_CONV_LAYERS, CONV_WIDTH, BRANCH_W), CONV_WIDTH ** -0.5),
        'conv_dw_b': nrm(ks[10], (N_CONV_LAYERS, BRANCH_W), 0.01),
        'conv_ln_g': 1.0 + nrm(ks[11], (N_CONV_LAYERS, BRANCH_W), 0.02),
        'conv_ln_b': nrm(ks[12], (N_CONV_LAYERS, BRANCH_W), 0.01),
        'conv_w_out': nrm(ks[13], (N_CONV_LAYERS, BRANCH_W, D_MODEL), BRANCH_W ** -0.5),
        'attn_w_in': nrm(ks[14], (N_ATTN_LAYERS, D_MODEL, Q_W + 2 * KV_W + BRANCH_W), D_MODEL ** -0.5),
        'attn_lambda_q1': nrm(ks[15], (N_ATTN_LAYERS, HEAD_DIM), 0.1),
        'attn_lambda_k1': nrm(ks[16], (N_ATTN_LAYERS, HEAD_DIM), 0.1),
        'attn_lambda_q2': nrm(ks[17], (N_ATTN_LAYERS, HEAD_DIM), 0.1),
        'attn_lambda_k2': nrm(ks[18], (N_ATTN_LAYERS, HEAD_DIM), 0.1),
        'attn_subln_g': 1.0 + nrm(ks[19], (N_ATTN_LAYERS, VAL_DIM), 0.02),
        'attn_w_out': nrm(ks[20], (N_ATTN_LAYERS, BRANCH_W, D_MODEL), BRANCH_W ** -0.5),
    }


def reference(x_prompt, x_sample, state_conv, cache_k, cache_v, page_table, norm_g, final_norm_g,
              conv_w_in, conv_dw_w, conv_dw_b, conv_ln_g, conv_ln_b, conv_w_out,
              attn_w_in, attn_lambda_q1, attn_lambda_k1, attn_lambda_q2, attn_lambda_k2,
              attn_subln_g, attn_w_out):
    yp, ys = x_prompt, x_sample
    conv_p, conv_s, k_p, v_p, k_s, v_s = [], [], [], [], [], []
    for i in range(DEPTH):
        j = i // N_MIXERS
        hp = rmsnorm(yp, norm_g[i])
        hs = rmsnorm(ys, norm_g[i])
        if i % N_MIXERS == 0:
            cw = (conv_w_in[j], conv_dw_w[j], conv_dw_b[j], conv_ln_g[j], conv_ln_b[j], conv_w_out[j])
            zeros = jnp.zeros((hp.shape[0], CONV_WIDTH - 1, BRANCH_W), hp.dtype)
            op, sp = conv_mixer(hp, zeros, *cw)
            os_, ss = conv_mixer(hs, state_conv[j], *cw)
            conv_p.append(sp)
            conv_s.append(ss)
        else:
            lam_init = 0.8 - 0.6 * math.exp(-0.3 * i)
            lam = diff_lambda(attn_lambda_q1[j], attn_lambda_k1[j], attn_lambda_q2[j], attn_lambda_k2[j], lam_init)
            op, kp, vp = attn_prompt(hp, attn_w_in[j], lam, attn_subln_g[j], lam_init, attn_w_out[j])
            os_, kq, vq = attn_sample(hs, cache_k[j], cache_v[j], page_table, attn_w_in[j], lam,
                                      attn_subln_g[j], lam_init, attn_w_out[j])
            k_p.append(kp)
            v_p.append(vp)
            k_s.append(kq)
            v_s.append(vq)
        yp = yp + op
        ys = ys + os_
    y_prompt = rmsnorm(yp, final_norm_g)
    y_sample = rmsnorm(ys, final_norm_g)
    return (y_prompt, y_sample, jnp.stack(conv_p), jnp.stack(conv_s),
            jnp.stack(k_p), jnp.stack(v_p), jnp.stack(k_s), jnp.stack(v_s))
```

```python
import functools
import math

import jax
import jax.numpy as jnp
from jax import lax
from jax.experimental import pallas as pl
from jax.experimental.pallas import tpu as pltpu

F32 = jnp.float32
BF16 = jnp.bfloat16

N_MIXERS = 2
CONV_WIDTH = 31
HEAD_DIM = 64
VAL_DIM = 2 * HEAD_DIM
GROUP = 4
ROT_DIM = HEAD_DIM // 4
ROPE_THETA = 500000.0
RMS_EPS = 1e-6
LN_EPS = 1e-5
SCALE = HEAD_DIM ** -0.5
PAGE_SIZE = 128

LANES = 128
HALO = 32
VMEM_LIMIT = 56 * 1024 * 1024


def _cparams(sem):
    return pltpu.CompilerParams(dimension_semantics=sem, vmem_limit_bytes=VMEM_LIMIT)


def _const_spec(shape):
    nd = len(shape)
    return pl.BlockSpec(shape, lambda *_: (0,) * nd)


def _rms(x, g):
    return x * lax.rsqrt(jnp.mean(x * x, axis=-1, keepdims=True) + RMS_EPS) * g


def _sigmoid(x):
    return 1.0 / (1.0 + jnp.exp(-x))


def _conv_tail(cb_ref, m_ref, gate, lng_ref, lnb_ref, n_chunks):
    tot = cb_ref[0]
    for c in range(1, n_chunks):
        tot = tot + cb_ref[c]
    width = n_chunks * LANES
    mu = jnp.sum(tot, axis=-1, keepdims=True) / width
    sq = None
    for c in range(n_chunks):
        d = cb_ref[c] - mu
        sq = d * d if sq is None else sq + d * d
    rstd = lax.rsqrt(jnp.sum(sq, axis=-1, keepdims=True) / width + LN_EPS)
    for c in range(n_chunks):
        sl = slice(c * LANES, (c + 1) * LANES)
        cn = (cb_ref[c] - mu) * rstd * lng_ref[:, sl] + lnb_ref[:, sl]
        act = cn * _sigmoid(cn)
        g = gate[:, sl]
        m_ref[:, sl] = (act * (g * _sigmoid(g))).astype(BF16)


def _conv_prompt_kernel(x_ref, ng_ref, win_ref, dww_ref, dwb_ref, lng_ref, lnb_ref, wout_ref,
                        y_ref, st_ref, upad_ref, cb_ref, m_ref, *, tm, nt, e):
    t = pl.program_id(1)
    n_chunks = e // LANES

    @pl.when(t == 0)
    def _():
        upad_ref[:, 0:HALO, :] = jnp.zeros((n_chunks, HALO, LANES), F32)

    x = x_ref[...]
    h = _rms(x, ng_ref[...]).astype(BF16)
    a = jnp.dot(h, win_ref[:, 0:e], preferred_element_type=F32)
    b = jnp.dot(h, win_ref[:, e:2 * e], preferred_element_type=F32)
    u = a * _sigmoid(b)
    for c in range(n_chunks):
        upad_ref[c, HALO:HALO + tm, :] = u[:, c * LANES:(c + 1) * LANES]

    off = HALO - (CONV_WIDTH - 1)

    def chunk(c, carry):
        acc = jnp.broadcast_to(dwb_ref[c], (tm, LANES))
        for k in range(CONV_WIDTH):
            acc = acc + dww_ref[c, k:k + 1, :] * upad_ref[c, off + k:off + k + tm, :]
        cb_ref[c] = acc
        return carry

    lax.fori_loop(0, n_chunks, chunk, 0)

    gate = jnp.dot(h, win_ref[:, 2 * e:3 * e], preferred_element_type=F32)
    _conv_tail(cb_ref, m_ref, gate, lng_ref, lnb_ref, n_chunks)
    y_ref[...] = jnp.dot(m_ref[...], wout_ref[...], preferred_element_type=F32) + x

    @pl.when(t == nt - 1)
    def _():
        for c in range(n_chunks):
            st_ref[0, :, c * LANES:(c + 1) * LANES] = upad_ref[c, tm + off:tm + HALO, :]

    upad_ref[:, 0:HALO, :] = upad_ref[:, tm:tm + HALO, :]


def _conv_prompt(x, ng, win, dww3, dwb3, lng, lnb, wout, *, batch, seq, tm):
    n, d = x.shape
    e = wout.shape[0]
    nt = seq // tm
    n_chunks = e // LANES
    kern = functools.partial(_conv_prompt_kernel, tm=tm, nt=nt, e=e)
    return pl.pallas_call(
        kern,
        grid=(batch, nt),
        in_specs=[
            pl.BlockSpec((tm, d), lambda b, t: (b * nt + t, 0)),
            _const_spec((1, d)),
            _const_spec(win.shape),
            _const_spec(dww3.shape),
            _const_spec(dwb3.shape),
            _const_spec((1, e)),
            _const_spec((1, e)),
            _const_spec(wout.shape),
        ],
        out_specs=[
            pl.BlockSpec((tm, d), lambda b, t: (b * nt + t, 0)),
            pl.BlockSpec((1, CONV_WIDTH - 1, e), lambda b, t: (b, 0, 0)),
        ],
        out_shape=[
            jax.ShapeDtypeStruct((n, d), F32),
            jax.ShapeDtypeStruct((batch, CONV_WIDTH - 1, e), F32),
        ],
        scratch_shapes=[
            pltpu.VMEM((n_chunks, HALO + tm, LANES), F32),
            pltpu.VMEM((n_chunks, tm, LANES), F32),
            pltpu.VMEM((tm, e), BF16),
        ],
        compiler_params=_cparams(("arbitrary", "arbitrary")),
        name="conv_prompt",
    )(x, ng, win, dww3, dwb3, lng, lnb, wout)


def _conv_sample_kernel(x_ref, ng_ref, win_ref, prev_ref, dww_ref, dwb_ref, lng_ref, lnb_ref, wout_ref,
                        y_ref, st_ref, u_ref, cb_ref, m_ref, *, nseq, ntok, e):
    n_chunks = e // LANES
    hist = CONV_WIDTH - 1
    x = x_ref[...]
    h = _rms(x, ng_ref[...]).astype(BF16)
    a = jnp.dot(h, win_ref[:, 0:e], preferred_element_type=F32)
    b = jnp.dot(h, win_ref[:, e:2 * e], preferred_element_type=F32)
    u_ref[...] = a * _sigmoid(b)

    def row(r, sl):
        if r < hist:
            return prev_ref[r, :, sl]
        return u_ref[(r - hist) * nseq:(r - hist + 1) * nseq, sl]

    for c in range(n_chunks):
        sl = slice(c * LANES, (c + 1) * LANES)
        for t in range(ntok):
            acc = jnp.broadcast_to(dwb_ref[c], (nseq, LANES))
            for k in range(CONV_WIDTH):
                acc = acc + dww_ref[c, k:k + 1, :] * row(t + k, sl)
            cb_ref[c, t * nseq:(t + 1) * nseq, :] = acc
        for r in range(hist):
            st_ref[r, :, sl] = row(r + ntok, sl)

    gate = jnp.dot(h, win_ref[:, 2 * e:3 * e], preferred_element_type=F32)
    _conv_tail(cb_ref, m_ref, gate, lng_ref, lnb_ref, n_chunks)
    y_ref[...] = jnp.dot(m_ref[...], wout_ref[...], preferred_element_type=F32) + x


def _conv_sample(x, ng, win, prev_t, dww3, dwb3, lng, lnb, wout, *, nseq, ntok):
    n, d = x.shape
    e = wout.shape[0]
    n_chunks = e // LANES
    kern = functools.partial(_conv_sample_kernel, nseq=nseq, ntok=ntok, e=e)
    args = (x, ng, win, prev_t, dww3, dwb3, lng, lnb, wout)
    return pl.pallas_call(
        kern,
        grid=(1,),
        in_specs=[_const_spec(a.shape) for a in args],
        out_specs=[_const_spec((n, d)), _const_spec(prev_t.shape)],
        out_shape=[jax.ShapeDtypeStruct((n, d), F32), jax.ShapeDtypeStruct(prev_t.shape, F32)],
        scratch_shapes=[
            pltpu.VMEM((n, e), F32),
            pltpu.VMEM((n_chunks, n, LANES), F32),
            pltpu.VMEM((n, e), BF16),
        ],
        compiler_params=_cparams(("arbitrary",)),
        name="conv_sample",
    )(*args)


def _rope_tile(x, cos, sin_up, sin_dn):
    return (x * cos + pltpu.roll(x, LANES - ROT_DIM // 2, axis=1) * sin_up
            + pltpu.roll(x, ROT_DIM // 2, axis=1) * sin_dn)


def _attn_proj_kernel(x_ref, ng_ref, win_ref, cos_ref, sup_ref, sdn_ref,
                      q_ref, kf_ref, kb_ref, vf_ref, vb_ref, gs_ref, *, qw, kvw, transpose_v):
    x = x_ref[...]
    h = _rms(x, ng_ref[...]).astype(BF16)
    cos, sup, sdn = cos_ref[...], sup_ref[...], sdn_ref[...]
    q = jnp.dot(h, win_ref[:, 0:qw], preferred_element_type=F32)
    for j in range(qw // LANES):
        sl = slice(j * LANES, (j + 1) * LANES)
        q_ref[:, sl] = (_rope_tile(q[:, sl], cos, sup, sdn) * SCALE).astype(BF16)
    k = jnp.dot(h, win_ref[:, qw:qw + kvw], preferred_element_type=F32)
    for j in range(kvw // LANES):
        sl = slice(j * LANES, (j + 1) * LANES)
        kr = _rope_tile(k[:, sl], cos, sup, sdn)
        kf_ref[:, sl] = kr
        kb_ref[:, sl] = kr.astype(BF16)
    v = jnp.dot(h, win_ref[:, qw + kvw:qw + 2 * kvw], preferred_element_type=F32)
    vf_ref[...] = v
    if transpose_v:
        vt = v.T.astype(BF16)
        for j in range(kvw // LANES):
            vb_ref[0, 0, j] = vt[j * LANES:(j + 1) * LANES, :]
    else:
        vb_ref[...] = v.astype(BF16)
    g = jnp.dot(h, win_ref[:, qw + 2 * kvw:], preferred_element_type=F32)
    gs_ref[...] = (g * _sigmoid(g)).astype(BF16)


def _attn_proj(x, ng, win, cos, sup, sdn, *, qw, kvw, tm, seq_tiles, transpose_v):
    n, d = x.shape
    nt = n // tm
    kvh = kvw // LANES
    gw = win.shape[1] - qw - 2 * kvw
    row = lambda w: pl.BlockSpec((tm, w), lambda i: (i, 0))
    tab = pl.BlockSpec((tm, LANES), lambda i: (i % seq_tiles, 0))
    if transpose_v:
        vb_spec = pl.BlockSpec((1, 1, kvh, LANES, tm), lambda i: (i // seq_tiles, i % seq_tiles, 0, 0, 0))
        vb_shape = jax.ShapeDtypeStruct((nt // seq_tiles, seq_tiles, kvh, LANES, tm), BF16)
    else:
        vb_spec = row(kvw)
        vb_shape = jax.ShapeDtypeStruct((n, kvw), BF16)
    kern = functools.partial(_attn_proj_kernel, qw=qw, kvw=kvw, transpose_v=transpose_v)
    return pl.pallas_call(
        kern,
        grid=(nt,),
        in_specs=[row(d), _const_spec((1, d)), _const_spec(win.shape), tab, tab, tab],
        out_specs=[row(qw), row(kvw), row(kvw), row(kvw), vb_spec, row(gw)],
        out_shape=[
            jax.ShapeDtypeStruct((n, qw), BF16),
            jax.ShapeDtypeStruct((n, kvw), F32),
            jax.ShapeDtypeStruct((n, kvw), BF16),
            jax.ShapeDtypeStruct((n, kvw), F32),
            vb_shape,
            jax.ShapeDtypeStruct((n, gw), BF16),
        ],
        compiler_params=_cparams(("arbitrary",)),
        name="attn_proj",
    )(x, ng, win, cos, sup, sdn)


def _diff_lambda(lq1_ref, lk1_ref, lq2_ref, lk2_ref, lam_init):
    s1 = jnp.sum(lq1_ref[...] * lk1_ref[...], axis=-1, keepdims=True)
    s2 = jnp.sum(lq2_ref[...] * lk2_ref[...], axis=-1, keepdims=True)
    return jnp.exp(s1) - jnp.exp(s2) + lam_init


def _flash_prompt_kernel(q_ref, k_ref, vt_ref, lq1_ref, lk1_ref, lq2_ref, lk2_ref, sg_ref, o_ref,
                         qp_ref, m_ref, l_ref, acc_ref, *, tq, sb, ct, lam_init):
    qi = pl.program_id(2)
    r = 2 * GROUP * tq
    lane = lax.broadcasted_iota(jnp.int32, (tq, LANES), 1)
    zero = jnp.zeros((tq, LANES), BF16)
    for g in range(GROUP):
        qg = q_ref[:, g * LANES:(g + 1) * LANES]
        qp_ref[(2 * g) * tq:(2 * g + 1) * tq, :] = jnp.where(lane < HEAD_DIM, qg, zero)
        qp_ref[(2 * g + 1) * tq:(2 * g + 2) * tq, :] = jnp.where(lane >= HEAD_DIM, qg, zero)
    m_ref[...] = jnp.full((1, r), -jnp.inf, F32)
    l_ref[...] = jnp.zeros((1, r), F32)
    acc_ref[...] = jnp.zeros((VAL_DIM, r), F32)

    def step(j, masked):
        kb = k_ref[pl.ds(pl.multiple_of(j * sb, sb), sb), :]
        vt = vt_ref[0, j, 0]
        for c in range(r // ct):
            cs = slice(c * ct, (c + 1) * ct)
            s = lax.dot_general(kb, qp_ref[cs, :], (((1,), (1,)), ((), ())), preferred_element_type=F32)
            if masked:
                key = j * sb + lax.broadcasted_iota(jnp.int32, (sb, ct), 0)
                tok = qi * tq + lax.rem(c * ct + lax.broadcasted_iota(jnp.int32, (sb, ct), 1), tq)
                s = jnp.where(key <= tok, s, -jnp.inf)
            m_old = m_ref[:, cs]
            m_new = jnp.maximum(m_old, jnp.max(s, axis=0, keepdims=True))
            alpha = jnp.exp(m_old - m_new)
            p = jnp.exp(s - m_new)
            l_ref[:, cs] = alpha * l_ref[:, cs] + jnp.sum(p, axis=0, keepdims=True)
            pv = jnp.dot(vt, p.astype(BF16), preferred_element_type=F32)
            acc_ref[:, cs] = alpha * acc_ref[:, cs] + pv
            m_ref[:, cs] = m_new

    n_full = (qi * tq) // sb

    def body(j, carry):
        step(j, False)
        return carry

    lax.fori_loop(0, n_full, body, 0)
    step(n_full, True)

    lam = _diff_lambda(lq1_ref, lk1_ref, lq2_ref, lk2_ref, lam_init)
    for g in range(GROUP):
        s1 = slice((2 * g) * tq, (2 * g + 1) * tq)
        s2 = slice((2 * g + 1) * tq, (2 * g + 2) * tq)
        o1 = acc_ref[:, s1] / l_ref[:, s1]
        o2 = acc_ref[:, s2] / l_ref[:, s2]
        d = o1 - lam * o2
        dn = d * lax.rsqrt(jnp.mean(d * d, axis=0, keepdims=True) + RMS_EPS)
        o_ref[:, g * LANES:(g + 1) * LANES] = ((dn.T * sg_ref[...]) * (1.0 - lam_init)).astype(BF16)


def _flash_prompt(q, kb, vt5, lams, sg, *, batch, seq, tq, ct, lam_init):
    n, qw = q.shape
    kvh = kb.shape[1] // LANES
    sb = vt5.shape[-1]
    nq = seq // tq
    r = 2 * GROUP * tq
    kern = functools.partial(_flash_prompt_kernel, tq=tq, sb=sb, ct=ct, lam_init=lam_init)
    vec = _const_spec((1, HEAD_DIM))
    return pl.pallas_call(
        kern,
        grid=(batch, kvh, nq),
        in_specs=[
            pl.BlockSpec((tq, GROUP * LANES), lambda b, h, i: (b * nq + i, h)),
            pl.BlockSpec((seq, LANES), lambda b, h, i: (b, h)),
            pl.BlockSpec((1, seq // sb, 1, LANES, sb), lambda b, h, i: (b, 0, h, 0, 0)),
            vec, vec, vec, vec,
            _const_spec((1, VAL_DIM)),
        ],
        out_specs=pl.BlockSpec((tq, GROUP * LANES), lambda b, h, i: (b * nq + i, h)),
        out_shape=jax.ShapeDtypeStruct((n, qw), BF16),
        scratch_shapes=[
            pltpu.VMEM((r, LANES), BF16),
            pltpu.VMEM((1, r), F32),
            pltpu.VMEM((1, r), F32),
            pltpu.VMEM((VAL_DIM, r), F32),
        ],
        compiler_params=_cparams(("arbitrary", "arbitrary", "arbitrary")),
        name="flash_prompt",
    )(q, kb, vt5, *lams, sg)


def _flash_sample_kernel(pt_ref, q_ref, *refs, ppc, ntok, lam_init):
    del pt_ref
    k_refs = refs[0:ppc]
    v_refs = refs[ppc:2 * ppc]
    kn_ref, vn_ref, lq1_ref, lk1_ref, lq2_ref, lk2_ref, sg_ref, o_ref, m_ref, l_ref, acc_ref = refs[2 * ppc:]
    c = pl.program_id(1)
    rows = q_ref.shape[1]
    kvh = q_ref.shape[2] // LANES
    rpk = rows // kvh

    @pl.when(c == 0)
    def _():
        m_ref[...] = jnp.full(m_ref.shape, -jnp.inf, F32)
        l_ref[...] = jnp.zeros(l_ref.shape, F32)
        acc_ref[...] = jnp.zeros(acc_ref.shape, F32)

    q = q_ref[0]

    def update(s_list, v_list):
        s = jnp.concatenate(s_list, axis=1) if len(s_list) > 1 else s_list[0]
        m_old = m_ref[...]
        m_new = jnp.maximum(m_old, jnp.max(s, axis=1, keepdims=True))
        alpha = jnp.exp(m_old - m_new)
        p = jnp.exp(s - m_new)
        l_ref[...] = alpha * l_ref[...] + jnp.sum(p, axis=1, keepdims=True)
        pb = p.astype(BF16)
        pv = None
        for i, v in enumerate(v_list):
            d = jnp.dot(pb[:, i * PAGE_SIZE:(i + 1) * PAGE_SIZE], v, preferred_element_type=F32)
            pv = d if pv is None else pv + d
        acc_ref[...] = alpha * acc_ref[...] + pv
        m_ref[...] = m_new

    nt_dims = (((1,), (1,)), ((), ()))
    s_list = [lax.dot_general(q, k_refs[i][0, 0].astype(BF16), nt_dims, preferred_element_type=F32)
              for i in range(ppc)]
    update(s_list, [v_refs[i][0, 0].astype(BF16) for i in range(ppc)])

    @pl.when(c == pl.num_programs(1) - 1)
    def _():
        s = lax.dot_general(q, kn_ref[0], nt_dims, preferred_element_type=F32)
        key = lax.broadcasted_iota(jnp.int32, s.shape, 1)
        tok = lax.rem(lax.broadcasted_iota(jnp.int32, s.shape, 0), ntok)
        update([jnp.where(key <= tok, s, -jnp.inf)], [vn_ref[0]])
        lam = _diff_lambda(lq1_ref, lk1_ref, lq2_ref, lk2_ref, lam_init)
        half = rpk // 2
        for h in range(kvh):
            blk = (acc_ref[h * rpk:(h + 1) * rpk, h * LANES:(h + 1) * LANES]
                   / l_ref[h * rpk:(h + 1) * rpk, :])
            d = blk[0:half] - lam * blk[half:rpk]
            dn = d * lax.rsqrt(jnp.mean(d * d, axis=-1, keepdims=True) + RMS_EPS)
            o_ref[0, h * half:(h + 1) * half, :] = (dn * sg_ref[...]) * (1.0 - lam_init)


def _flash_sample(page_table, qblk, cache_k4, cache_v4, layer, knew, vnew, lams, sg, *, ppc, ntok, lam_init):
    nseq, rows, kw = qblk.shape
    n_pages = page_table.shape[1]
    nch = n_pages // ppc
    kvh = kw // LANES
    kern = functools.partial(_flash_sample_kernel, ppc=ppc, ntok=ntok, lam_init=lam_init)

    def page_spec(i):
        return pl.BlockSpec((1, 1, PAGE_SIZE, kw), lambda n, c, pt: (layer, pt[n, c * ppc + i], 0, 0))

    seq_spec = pl.BlockSpec((1, PAGE_SIZE, kw), lambda n, c, pt: (n, 0, 0))
    vec = pl.BlockSpec((1, HEAD_DIM), lambda n, c, pt: (0, 0))
    out_rows = rows // 2
    grid_spec = pltpu.PrefetchScalarGridSpec(
        num_scalar_prefetch=1,
        grid=(nseq, nch),
        in_specs=([pl.BlockSpec((1, rows, kw), lambda n, c, pt: (n, 0, 0))]
                  + [page_spec(i) for i in range(ppc)] + [page_spec(i) for i in range(ppc)]
                  + [seq_spec, seq_spec, vec, vec, vec, vec,
                     pl.BlockSpec((1, VAL_DIM), lambda n, c, pt: (0, 0))]),
        out_specs=pl.BlockSpec((1, out_rows, VAL_DIM), lambda n, c, pt: (n, 0, 0)),
        scratch_shapes=[
            pltpu.VMEM((rows, 1), F32),
            pltpu.VMEM((rows, 1), F32),
            pltpu.VMEM((rows, kw), F32),
        ],
    )
    return pl.pallas_call(
        kern,
        grid_spec=grid_spec,
        out_shape=jax.ShapeDtypeStruct((nseq, out_rows, VAL_DIM), F32),
        compiler_params=_cparams(("arbitrary", "arbitrary")),
        name="flash_sample",
    )(page_table, qblk, *([cache_k4] * ppc), *([cache_v4] * ppc), knew, vnew, *lams, sg)


def _attn_out_kernel(o_ref, gs_ref, w_ref, x_ref, *rest, final):
    if final:
        fg_ref, y_ref = rest
    else:
        (y_ref,) = rest
    y = jnp.dot(o_ref[...] * gs_ref[...], w_ref[...], preferred_element_type=F32) + x_ref[...]
    if final:
        y = _rms(y, fg_ref[...])
    y_ref[...] = y


def _attn_out(o, gs, w, x, fg, *, tm):
    n, d = x.shape
    e = w.shape[0]
    final = fg is not None
    row = lambda wd: pl.BlockSpec((tm, wd), lambda i: (i, 0))
    in_specs = [row(e), row(e), _const_spec(w.shape), row(d)]
    args = [o, gs, w, x]
    if final:
        in_specs.append(_const_spec((1, d)))
        args.append(fg)
    return pl.pallas_call(
        functools.partial(_attn_out_kernel, final=final),
        grid=(n // tm,),
        in_specs=in_specs,
        out_specs=row(d),
        out_shape=jax.ShapeDtypeStruct((n, d), F32),
        compiler_params=_cparams(("arbitrary",)),
        name="attn_out",
    )(*args)


def _final_norm_kernel(x_ref, g_ref, y_ref):
    y_ref[...] = _rms(x_ref[...], g_ref[...])


def _final_norm(x, g, *, tm):
    n, d = x.shape
    row = pl.BlockSpec((tm, d), lambda i: (i, 0))
    return pl.pallas_call(
        _final_norm_kernel, grid=(n // tm,), in_specs=[row, _const_spec((1, d))], out_specs=row,
        out_shape=jax.ShapeDtypeStruct((n, d), F32), compiler_params=_cparams(("arbitrary",)),
        name="final_norm",
    )(x, g)


def _rope_tables(pos):
    half = ROT_DIM // 2
    inv_freq = ROPE_THETA ** (-jnp.arange(half, dtype=F32) / half)
    ang = pos.astype(F32)[:, None] * inv_freq[None, :]
    cos, sin = jnp.cos(ang), jnp.sin(ang)
    n = pos.shape[0]
    pad = jnp.zeros((n, HEAD_DIM - ROT_DIM), F32)
    zero = jnp.zeros((n, half), F32)
    c64 = jnp.concatenate([cos, cos, pad + 1.0], axis=1)
    up64 = jnp.concatenate([-sin, zero, pad], axis=1)
    dn64 = jnp.concatenate([zero, sin, pad], axis=1)
    rep = lambda t: jnp.concatenate([t, t], axis=1)
    return rep(c64), rep(up64), rep(dn64)


def _pick_tile(n, pref):
    t = min(n, pref)
    while n % t:
        t //= 2
    return t


def kernel(x_prompt, x_sample, state_conv, cache_k, cache_v, page_table, norm_g, final_norm_g,
           conv_w_in, conv_dw_w, conv_dw_b, conv_ln_g, conv_ln_b, conv_w_out,
           attn_w_in, attn_lambda_q1, attn_lambda_k1, attn_lambda_q2, attn_lambda_k2,
           attn_subln_g, attn_w_out):
    batch, seq, d = x_prompt.shape
    nseq, ntok, _ = x_sample.shape
    depth = norm_g.shape[0]
    e = conv_w_out.shape[1]
    n_heads = e // VAL_DIM
    kvh = n_heads // GROUP
    qw = n_heads * 2 * HEAD_DIM
    kvw = kvh * 2 * HEAD_DIM
    n_chunks = e // LANES
    hist = CONV_WIDTH - 1
    n_pages = page_table.shape[1]
    past = n_pages * PAGE_SIZE

    tm_conv = _pick_tile(seq, 256)
    tm_proj = _pick_tile(seq, 512)
    tq = _pick_tile(seq, 128)
    ct = 512
    ppc = _pick_tile(n_pages, 8)

    yp = x_prompt.reshape(batch * seq, d)
    ys = jnp.swapaxes(x_sample, 0, 1).reshape(ntok * nseq, d)
    ns = ntok * nseq

    tabs_p = _rope_tables(jnp.arange(seq, dtype=jnp.int32))
    tabs_s = _rope_tables(past + jnp.repeat(jnp.arange(ntok, dtype=jnp.int32), nseq))

    cache_k4 = cache_k.reshape(cache_k.shape[0], cache_k.shape[1], PAGE_SIZE, kvw)
    cache_v4 = cache_v.reshape(cache_v.shape[0], cache_v.shape[1], PAGE_SIZE, kvw)
    eye_h = jnp.eye(kvh, dtype=BF16)
    eye_c = jnp.eye(2, dtype=BF16)

    conv_p, conv_s, k_p, v_p, k_s, v_s = [], [], [], [], [], []
    for i in range(depth):
        j = i // N_MIXERS
        ng = norm_g[i].reshape(1, d)
        last = i == depth - 1
        if i % N_MIXERS == 0:
            win = conv_w_in[j].astype(BF16)
            wout = conv_w_out[j].astype(BF16)
            dww3 = jnp.pad(conv_dw_w[j], ((0, HALO - CONV_WIDTH), (0, 0))).reshape(HALO, n_chunks, LANES)
            dww3 = jnp.swapaxes(dww3, 0, 1)
            dwb3 = conv_dw_b[j].reshape(n_chunks, 1, LANES)
            lng = conv_ln_g[j].reshape(1, e)
            lnb = conv_ln_b[j].reshape(1, e)
            yp, sp = _conv_prompt(yp, ng, win, dww3, dwb3, lng, lnb, wout, batch=batch, seq=seq, tm=tm_conv)
            prev_t = jnp.swapaxes(state_conv[j], 0, 1)
            ys, ss_t = _conv_sample(ys, ng, win, prev_t, dww3, dwb3, lng, lnb, wout, nseq=nseq, ntok=ntok)
            conv_p.append(sp)
            conv_s.append(jnp.swapaxes(ss_t, 0, 1))
            if last:
                yp = _final_norm(yp, final_norm_g.reshape(1, d), tm=tm_proj)
                ys = _final_norm(ys, final_norm_g.reshape(1, d), tm=ns)
        else:
            lam_init = 0.8 - 0.6 * math.exp(-0.3 * i)
            win = attn_w_in[j].astype(BF16)
            wout = attn_w_out[j].astype(BF16)
            lams = [a[j].reshape(1, HEAD_DIM) for a in
                    (attn_lambda_q1, attn_lambda_k1, attn_lambda_q2, attn_lambda_k2)]
            sg = attn_subln_g[j].reshape(1, VAL_DIM)
            fg = final_norm_g.reshape(1, d) if last else None

            q, kf, kb, vf, vt5, gs = _attn_proj(yp, ng, win, *tabs_p, qw=qw, kvw=kvw, tm=tm_proj,
                                                seq_tiles=seq // tm_proj, transpose_v=True)
            o = _flash_prompt(q, kb, vt5, lams, sg, batch=batch, seq=seq, tq=tq, ct=ct, lam_init=lam_init)
            yp = _attn_out(o, gs, wout, yp, fg, tm=tm_proj)
            k_p.append(kf.reshape(batch, seq, kvh, 2 * HEAD_DIM))
            v_p.append(vf.reshape(batch, seq, kvh, VAL_DIM))

            q, kf, kb, vf, vb, gs = _attn_proj(ys, ng, win, *tabs_s, qw=qw, kvw=kvw, tm=ns,
                                               seq_tiles=1, transpose_v=False)
            q6 = q.reshape(ntok, nseq, kvh, GROUP, 2, HEAD_DIM).transpose(1, 2, 4, 3, 0, 5)
            q6 = q6.reshape(nseq, kvh, 2, GROUP * ntok, HEAD_DIM)
            qblk = (q6[:, :, :, :, None, None, :] * eye_h[None, :, None, None, :, None, None]
                    * eye_c[None, None, :, None, None, :, None])
            qblk = qblk.reshape(nseq, kvh * 2 * GROUP * ntok, kvw)
            pad_new = lambda a: jnp.pad(jnp.swapaxes(a.reshape(ntok, nseq, kvw), 0, 1),
                                        ((0, 0), (0, PAGE_SIZE - ntok), (0, 0)))
            od = _flash_sample(page_table, qblk, cache_k4, cache_v4, j, pad_new(kb), pad_new(vb), lams, sg,
                               ppc=ppc, ntok=ntok, lam_init=lam_init)
            o = od.reshape(nseq, kvh, GROUP, ntok, VAL_DIM).transpose(3, 0, 1, 2, 4).reshape(ns, e)
            ys = _attn_out(o.astype(BF16), gs, wout, ys, fg, tm=ns)
            to_rows = lambda a: jnp.swapaxes(a.reshape(ntok, nseq, kvh, VAL_DIM), 0, 1)
            k_s.append(to_rows(kf))
            v_s.append(to_rows(vf))

    y_prompt = yp.reshape(batch, seq, d)
    y_sample = jnp.swapaxes(ys.reshape(ntok, nseq, d), 0, 1)
    return (y_prompt, y_sample, jnp.stack(conv_p), jnp.stack(conv_s),
            jnp.stack(k_p), jnp.stack(v_p), jnp.stack(k_s), jnp.stack(v_s))
```

```python
import functools
import math

import jax
import jax.numpy as jnp
from jax import lax
from jax.experimental import pallas as pl
from jax.experimental.pallas import tpu as pltpu

F32 = jnp.float32
BF16 = jnp.bfloat16

N_MIXERS = 2
CONV_WIDTH = 31
HEAD_DIM = 64
VAL_DIM = 2 * HEAD_DIM
GROUP = 4
ROT_DIM = HEAD_DIM // 4
ROPE_THETA = 500000.0
RMS_EPS = 1e-6
LN_EPS = 1e-5
SCALE = HEAD_DIM ** -0.5
Q_SCALE = SCALE * math.log2(math.e)
PAGE_SIZE = 128

LANES = 128
HALO = 32
VMEM_LIMIT = 56 * 1024 * 1024


def _cparams(sem):
    return pltpu.CompilerParams(dimension_semantics=sem, vmem_limit_bytes=VMEM_LIMIT)


def _const_spec(shape):
    nd = len(shape)
    return pl.BlockSpec(shape, lambda *_: (0,) * nd)


def _rms(x, g):
    return x * lax.rsqrt(jnp.mean(x * x, axis=-1, keepdims=True) + RMS_EPS) * g


def _sigmoid(x):
    return 1.0 / (1.0 + jnp.exp(-x))


def _conv_tail(cb_ref, m_ref, gate, lng_ref, lnb_ref, n_chunks):
    tot = cb_ref[0]
    for c in range(1, n_chunks):
        tot = tot + cb_ref[c]
    width = n_chunks * LANES
    mu = jnp.sum(tot, axis=-1, keepdims=True) / width
    sq = None
    for c in range(n_chunks):
        d = cb_ref[c] - mu
        sq = d * d if sq is None else sq + d * d
    rstd = lax.rsqrt(jnp.sum(sq, axis=-1, keepdims=True) / width + LN_EPS)
    for c in range(n_chunks):
        sl = slice(c * LANES, (c + 1) * LANES)
        cn = (cb_ref[c] - mu) * rstd * lng_ref[:, sl] + lnb_ref[:, sl]
        act = cn * _sigmoid(cn)
        g = gate[:, sl]
        m_ref[:, sl] = (act * (g * _sigmoid(g))).astype(BF16)


def _conv_prompt_kernel(x_ref, ng_ref, win_ref, dww_ref, dwb_ref, lng_ref, lnb_ref, wout_ref,
                        y_ref, st_ref, upad_ref, cb_ref, m_ref, *, tm, nt, e):
    t = pl.program_id(1)
    n_chunks = e // LANES

    @pl.when(t == 0)
    def _():
        upad_ref[:, 0:HALO, :] = jnp.zeros((n_chunks, HALO, LANES), F32)

    x = x_ref[...]
    h = _rms(x, ng_ref[...]).astype(BF16)
    a = jnp.dot(h, win_ref[:, 0:e], preferred_element_type=F32)
    b = jnp.dot(h, win_ref[:, e:2 * e], preferred_element_type=F32)
    u = a * _sigmoid(b)
    for c in range(n_chunks):
        upad_ref[c, HALO:HALO + tm, :] = u[:, c * LANES:(c + 1) * LANES]

    off = HALO - (CONV_WIDTH - 1)

    def chunk(c, carry):
        acc = jnp.broadcast_to(dwb_ref[c], (tm, LANES))
        for k in range(CONV_WIDTH):
            acc = acc + dww_ref[c, k:k + 1, :] * upad_ref[c, off + k:off + k + tm, :]
        cb_ref[c] = acc
        return carry

    lax.fori_loop(0, n_chunks, chunk, 0)

    gate = jnp.dot(h, win_ref[:, 2 * e:3 * e], preferred_element_type=F32)
    _conv_tail(cb_ref, m_ref, gate, lng_ref, lnb_ref, n_chunks)
    y_ref[...] = jnp.dot(m_ref[...], wout_ref[...], preferred_element_type=F32) + x

    @pl.when(t == nt - 1)
    def _():
        for c in range(n_chunks):
            st_ref[0, :, c * LANES:(c + 1) * LANES] = upad_ref[c, tm + off:tm + HALO, :]

    upad_ref[:, 0:HALO, :] = upad_ref[:, tm:tm + HALO, :]


def _conv_prompt(x, ng, win, dww3, dwb3, lng, lnb, wout, *, batch, seq, tm):
    n, d = x.shape
    e = wout.shape[0]
    nt = seq // tm
    n_chunks = e // LANES
    kern = functools.partial(_conv_prompt_kernel, tm=tm, nt=nt, e=e)
    return pl.pallas_call(
        kern,
        grid=(batch, nt),
        in_specs=[
            pl.BlockSpec((tm, d), lambda b, t: (b * nt + t, 0)),
            _const_spec((1, d)),
            _const_spec(win.shape),
            _const_spec(dww3.shape),
            _const_spec(dwb3.shape),
            _const_spec((1, e)),
            _const_spec((1, e)),
            _const_spec(wout.shape),
        ],
        out_specs=[
            pl.BlockSpec((tm, d), lambda b, t: (b * nt + t, 0)),
            pl.BlockSpec((1, CONV_WIDTH - 1, e), lambda b, t: (b, 0, 0)),
        ],
        out_shape=[
            jax.ShapeDtypeStruct((n, d), F32),
            jax.ShapeDtypeStruct((batch, CONV_WIDTH - 1, e), F32),
        ],
        scratch_shapes=[
            pltpu.VMEM((n_chunks, HALO + tm, LANES), F32),
            pltpu.VMEM((n_chunks, tm, LANES), F32),
            pltpu.VMEM((tm, e), BF16),
        ],
        compiler_params=_cparams(("arbitrary", "arbitrary")),
        name="conv_prompt",
    )(x, ng, win, dww3, dwb3, lng, lnb, wout)


def _conv_sample_kernel(x_ref, ng_ref, win_ref, prev_ref, dww_ref, dwb_ref, lng_ref, lnb_ref, wout_ref,
                        y_ref, st_ref, u_ref, cb_ref, m_ref, *, nseq, ntok, e):
    n_chunks = e // LANES
    hist = CONV_WIDTH - 1
    x = x_ref[...]
    h = _rms(x, ng_ref[...]).astype(BF16)
    a = jnp.dot(h, win_ref[:, 0:e], preferred_element_type=F32)
    b = jnp.dot(h, win_ref[:, e:2 * e], preferred_element_type=F32)
    u_ref[...] = a * _sigmoid(b)

    def row(r, sl):
        if r < hist:
            return prev_ref[r, :, sl]
        return u_ref[(r - hist) * nseq:(r - hist + 1) * nseq, sl]

    for c in range(n_chunks):
        sl = slice(c * LANES, (c + 1) * LANES)
        for t in range(ntok):
            acc = jnp.broadcast_to(dwb_ref[c], (nseq, LANES))
            for k in range(CONV_WIDTH):
                acc = acc + dww_ref[c, k:k + 1, :] * row(t + k, sl)
            cb_ref[c, t * nseq:(t + 1) * nseq, :] = acc
        for r in range(hist):
            st_ref[r, :, sl] = row(r + ntok, sl)

    gate = jnp.dot(h, win_ref[:, 2 * e:3 * e], preferred_element_type=F32)
    _conv_tail(cb_ref, m_ref, gate, lng_ref, lnb_ref, n_chunks)
    y_ref[...] = jnp.dot(m_ref[...], wout_ref[...], preferred_element_type=F32) + x


def _conv_sample(x, ng, win, prev_t, dww3, dwb3, lng, lnb, wout, *, nseq, ntok):
    n, d = x.shape
    e = wout.shape[0]
    n_chunks = e // LANES
    kern = functools.partial(_conv_sample_kernel, nseq=nseq, ntok=ntok, e=e)
    args = (x, ng, win, prev_t, dww3, dwb3, lng, lnb, wout)
    return pl.pallas_call(
        kern,
        grid=(1,),
        in_specs=[_const_spec(a.shape) for a in args],
        out_specs=[_const_spec((n, d)), _const_spec(prev_t.shape)],
        out_shape=[jax.ShapeDtypeStruct((n, d), F32), jax.ShapeDtypeStruct(prev_t.shape, F32)],
        scratch_shapes=[
            pltpu.VMEM((n, e), F32),
            pltpu.VMEM((n_chunks, n, LANES), F32),
            pltpu.VMEM((n, e), BF16),
        ],
        compiler_params=_cparams(("arbitrary",)),
        name="conv_sample",
    )(*args)


def _rope_tile(x, cos, sin_up, sin_dn):
    return (x * cos + pltpu.roll(x, LANES - ROT_DIM // 2, axis=1) * sin_up
            + pltpu.roll(x, ROT_DIM // 2, axis=1) * sin_dn)


def _attn_proj_kernel(x_ref, ng_ref, win_ref, cos_ref, sup_ref, sdn_ref,
                      q_ref, kf_ref, kb_ref, vf_ref, vb_ref, gs_ref, *, qw, kvw, transpose_v):
    x = x_ref[...]
    h = _rms(x, ng_ref[...]).astype(BF16)
    cos, sup, sdn = cos_ref[...], sup_ref[...], sdn_ref[...]
    q = jnp.dot(h, win_ref[:, 0:qw], preferred_element_type=F32)
    for j in range(qw // LANES):
        sl = slice(j * LANES, (j + 1) * LANES)
        q_ref[:, sl] = (_rope_tile(q[:, sl], cos, sup, sdn) * Q_SCALE).astype(BF16)
    k = jnp.dot(h, win_ref[:, qw:qw + kvw], preferred_element_type=F32)
    for j in range(kvw // LANES):
        sl = slice(j * LANES, (j + 1) * LANES)
        kr = _rope_tile(k[:, sl], cos, sup, sdn)
        kf_ref[:, sl] = kr
        kb_ref[:, sl] = kr.astype(BF16)
    v = jnp.dot(h, win_ref[:, qw + kvw:qw + 2 * kvw], preferred_element_type=F32)
    vf_ref[...] = v
    if transpose_v:
        vt = v.T.astype(BF16)
        for j in range(kvw // LANES):
            vb_ref[0, 0, j] = vt[j * LANES:(j + 1) * LANES, :]
    else:
        vb_ref[...] = v.astype(BF16)
    g = jnp.dot(h, win_ref[:, qw + 2 * kvw:], preferred_element_type=F32)
    gs_ref[...] = (g * _sigmoid(g)).astype(BF16)


def _attn_proj(x, ng, win, cos, sup, sdn, *, qw, kvw, tm, seq_tiles, transpose_v):
    n, d = x.shape
    nt = n // tm
    kvh = kvw // LANES
    gw = win.shape[1] - qw - 2 * kvw
    row = lambda w: pl.BlockSpec((tm, w), lambda i: (i, 0))
    tab = pl.BlockSpec((tm, LANES), lambda i: (i % seq_tiles, 0))
    if transpose_v:
        vb_spec = pl.BlockSpec((1, 1, kvh, LANES, tm), lambda i: (i // seq_tiles, i % seq_tiles, 0, 0, 0))
        vb_shape = jax.ShapeDtypeStruct((nt // seq_tiles, seq_tiles, kvh, LANES, tm), BF16)
    else:
        vb_spec = row(kvw)
        vb_shape = jax.ShapeDtypeStruct((n, kvw), BF16)
    kern = functools.partial(_attn_proj_kernel, qw=qw, kvw=kvw, transpose_v=transpose_v)
    return pl.pallas_call(
        kern,
        grid=(nt,),
        in_specs=[row(d), _const_spec((1, d)), _const_spec(win.shape), tab, tab, tab],
        out_specs=[row(qw), row(kvw), row(kvw), row(kvw), vb_spec, row(gw)],
        out_shape=[
            jax.ShapeDtypeStruct((n, qw), BF16),
            jax.ShapeDtypeStruct((n, kvw), F32),
            jax.ShapeDtypeStruct((n, kvw), BF16),
            jax.ShapeDtypeStruct((n, kvw), F32),
            vb_shape,
            jax.ShapeDtypeStruct((n, gw), BF16),
        ],
        compiler_params=_cparams(("arbitrary",)),
        name="attn_proj",
    )(x, ng, win, cos, sup, sdn)


def _diff_lambda(lq1_ref, lk1_ref, lq2_ref, lk2_ref, lam_init):
    s1 = jnp.sum(lq1_ref[...] * lk1_ref[...], axis=-1, keepdims=True)
    s2 = jnp.sum(lq2_ref[...] * lk2_ref[...], axis=-1, keepdims=True)
    return jnp.exp(s1) - jnp.exp(s2) + lam_init


def _flash_prompt_kernel(q_ref, k_ref, vt_ref, lq1_ref, lk1_ref, lq2_ref, lk2_ref, sg_ref, o_ref,
                         qp_ref, m_ref, l_ref, acc_ref, s0_ref, s1_ref, x0_ref, x1_ref,
                         *, tq, sb, ct, lam_init):
    qi = pl.program_id(2)
    r = 2 * GROUP * tq
    lane = lax.broadcasted_iota(jnp.int32, (tq, LANES), 1)
    zero = jnp.zeros((tq, LANES), BF16)
    for g in range(GROUP):
        qg = q_ref[:, g * LANES:(g + 1) * LANES]
        qp_ref[(2 * g) * tq:(2 * g + 1) * tq, :] = jnp.where(lane < HEAD_DIM, qg, zero)
        qp_ref[(2 * g + 1) * tq:(2 * g + 2) * tq, :] = jnp.where(lane >= HEAD_DIM, qg, zero)
    m_ref[...] = jnp.full((1, r), -jnp.inf, F32)
    l_ref[...] = jnp.zeros((1, r), F32)
    acc_ref[...] = jnp.zeros((VAL_DIM, r), F32)

    n_ct = r // ct
    nt_dims = (((1,), (1,)), ((), ()))

    def scores_into(s_ref, x_ref, j):
        kb = k_ref[pl.ds(pl.multiple_of(j * sb, sb), sb), :]
        for c in range(n_ct):
            cs = slice(c * ct, (c + 1) * ct)
            s = lax.dot_general(kb, qp_ref[cs, :], nt_dims, preferred_element_type=F32)
            s_ref[:, cs] = s
            x_ref[:, cs] = jnp.max(s, axis=0, keepdims=True)

    def absorb(s_ref, x_ref, j, masked):
        vt = vt_ref[0, j, 0]
        for c in range(n_ct):
            cs = slice(c * ct, (c + 1) * ct)
            s = s_ref[:, cs]
            if masked:
                key = j * sb + lax.broadcasted_iota(jnp.int32, (sb, ct), 0)
                tok = qi * tq + lax.rem(c * ct + lax.broadcasted_iota(jnp.int32, (sb, ct), 1), tq)
                s = jnp.where(key <= tok, s, -jnp.inf)
                mx = jnp.max(s, axis=0, keepdims=True)
            else:
                mx = x_ref[:, cs]
            m_old = m_ref[:, cs]
            m_new = jnp.maximum(m_old, mx)
            alpha = jnp.exp2(m_old - m_new)
            p = jnp.exp2(s - m_new)
            l_ref[:, cs] = alpha * l_ref[:, cs] + jnp.sum(p, axis=0, keepdims=True)
            pv = jnp.dot(vt, p.astype(BF16), preferred_element_type=F32)
            acc_ref[:, cs] = alpha * acc_ref[:, cs] + pv
            m_ref[:, cs] = m_new

    n_full = (qi * tq) // sb
    scores_into(s0_ref, x0_ref, 0)

    def pair(jj, carry):
        j0 = 2 * jj
        scores_into(s1_ref, x1_ref, j0 + 1)
        absorb(s0_ref, x0_ref, j0, False)
        scores_into(s0_ref, x0_ref, j0 + 2)
        absorb(s1_ref, x1_ref, j0 + 1, False)
        return carry

    lax.fori_loop(0, n_full // 2, pair, 0)
    odd = lax.rem(n_full, 2) == 1

    @pl.when(odd)
    def _():
        scores_into(s1_ref, x1_ref, n_full)
        absorb(s0_ref, x0_ref, n_full - 1, False)
        absorb(s1_ref, x1_ref, n_full, True)

    @pl.when(jnp.logical_not(odd))
    def _():
        absorb(s0_ref, x0_ref, n_full, True)

    lam = _diff_lambda(lq1_ref, lk1_ref, lq2_ref, lk2_ref, lam_init)
    for g in range(GROUP):
        s1 = slice((2 * g) * tq, (2 * g + 1) * tq)
        s2 = slice((2 * g + 1) * tq, (2 * g + 2) * tq)
        o1 = acc_ref[:, s1] / l_ref[:, s1]
        o2 = acc_ref[:, s2] / l_ref[:, s2]
        d = o1 - lam * o2
        dn = d * lax.rsqrt(jnp.mean(d * d, axis=0, keepdims=True) + RMS_EPS)
        o_ref[:, g * LANES:(g + 1) * LANES] = ((dn.T * sg_ref[...]) * (1.0 - lam_init)).astype(BF16)


def _flash_prompt(q, kb, vt5, lams, sg, *, batch, seq, tq, ct, lam_init):
    n, qw = q.shape
    kvh = kb.shape[1] // LANES
    sb = vt5.shape[-1]
    nq = seq // tq
    r = 2 * GROUP * tq
    kern = functools.partial(_flash_prompt_kernel, tq=tq, sb=sb, ct=ct, lam_init=lam_init)
    vec = _const_spec((1, HEAD_DIM))
    return pl.pallas_call(
        kern,
        grid=(batch, kvh, nq),
        in_specs=[
            pl.BlockSpec((tq, GROUP * LANES), lambda b, h, i: (b * nq + i, h)),
            pl.BlockSpec((seq, LANES), lambda b, h, i: (b, h)),
            pl.BlockSpec((1, seq // sb, 1, LANES, sb), lambda b, h, i: (b, 0, h, 0, 0)),
            vec, vec, vec, vec,
            _const_spec((1, VAL_DIM)),
        ],
        out_specs=pl.BlockSpec((tq, GROUP * LANES), lambda b, h, i: (b * nq + i, h)),
        out_shape=jax.ShapeDtypeStruct((n, qw), BF16),
        scratch_shapes=[
            pltpu.VMEM((r, LANES), BF16),
            pltpu.VMEM((1, r), F32),
            pltpu.VMEM((1, r), F32),
            pltpu.VMEM((VAL_DIM, r), F32),
            pltpu.VMEM((sb, r), F32),
            pltpu.VMEM((sb, r), F32),
            pltpu.VMEM((1, r), F32),
            pltpu.VMEM((1, r), F32),
        ],
        compiler_params=_cparams(("arbitrary", "arbitrary", "arbitrary")),
        name="flash_prompt",
    )(q, kb, vt5, *lams, sg)


def _flash_sample_kernel(pt_ref, q_ref, *refs, ppc, ntok, lam_init):
    del pt_ref
    k_refs = refs[0:ppc]
    v_refs = refs[ppc:2 * ppc]
    kn_ref, vn_ref, lq1_ref, lk1_ref, lq2_ref, lk2_ref, sg_ref, o_ref, m_ref, l_ref, acc_ref = refs[2 * ppc:]
    c = pl.program_id(1)
    rows = q_ref.shape[1]
    kvh = q_ref.shape[2] // LANES
    rpk = rows // kvh

    @pl.when(c == 0)
    def _():
        m_ref[...] = jnp.full(m_ref.shape, -jnp.inf, F32)
        l_ref[...] = jnp.zeros(l_ref.shape, F32)
        acc_ref[...] = jnp.zeros(acc_ref.shape, F32)

    q = q_ref[0]

    def update(s_list, v_list):
        s = jnp.concatenate(s_list, axis=1) if len(s_list) > 1 else s_list[0]
        m_old = m_ref[...]
        m_new = jnp.maximum(m_old, jnp.max(s, axis=1, keepdims=True))
        alpha = jnp.exp2(m_old - m_new)
        p = jnp.exp2(s - m_new)
        l_ref[...] = alpha * l_ref[...] + jnp.sum(p, axis=1, keepdims=True)
        pb = p.astype(BF16)
        pv = None
        for i, v in enumerate(v_list):
            d = jnp.dot(pb[:, i * PAGE_SIZE:(i + 1) * PAGE_SIZE], v, preferred_element_type=F32)
            pv = d if pv is None else pv + d
        acc_ref[...] = alpha * acc_ref[...] + pv
        m_ref[...] = m_new

    def page(ref):
        heads = [ref[0, 0, pl.ds(h, PAGE_SIZE, stride=kvh), :] for h in range(kvh)]
        return jnp.concatenate(heads, axis=1).astype(BF16)

    nt_dims = (((1,), (1,)), ((), ()))
    s_list = [lax.dot_general(q, page(k_refs[i]), nt_dims, preferred_element_type=F32) for i in range(ppc)]
    update(s_list, [page(v_refs[i]) for i in range(ppc)])

    @pl.when(c == pl.num_programs(1) - 1)
    def _():
        s = lax.dot_general(q, kn_ref[0], nt_dims, preferred_element_type=F32)
        key = lax.broadcasted_iota(jnp.int32, s.shape, 1)
        tok = lax.rem(lax.broadcasted_iota(jnp.int32, s.shape, 0), ntok)
        update([jnp.where(key <= tok, s, -jnp.inf)], [vn_ref[0]])
        lam = _diff_lambda(lq1_ref, lk1_ref, lq2_ref, lk2_ref, lam_init)
        half = rpk // 2
        for h in range(kvh):
            blk = (acc_ref[h * rpk:(h + 1) * rpk, h * LANES:(h + 1) * LANES]
                   / l_ref[h * rpk:(h + 1) * rpk, :])
            d = blk[0:half] - lam * blk[half:rpk]
            dn = d * lax.rsqrt(jnp.mean(d * d, axis=-1, keepdims=True) + RMS_EPS)
            o_ref[0, h * half:(h + 1) * half, :] = (dn * sg_ref[...]) * (1.0 - lam_init)


def _flash_sample(page_table, qblk, cache_k4, cache_v4, layer, knew, vnew, lams, sg, *, ppc, ntok, lam_init):
    nseq, rows, kw = qblk.shape
    n_pages = page_table.shape[1]
    nch = n_pages // ppc
    kvh = kw // LANES
    kern = functools.partial(_flash_sample_kernel, ppc=ppc, ntok=ntok, lam_init=lam_init)

    def page_spec(i):
        return pl.BlockSpec((1, 1, PAGE_SIZE * kvh, LANES), lambda n, c, pt: (layer, pt[n, c * ppc + i], 0, 0))

    seq_spec = pl.BlockSpec((1, PAGE_SIZE, kw), lambda n, c, pt: (n, 0, 0))
    vec = pl.BlockSpec((1, HEAD_DIM), lambda n, c, pt: (0, 0))
    out_rows = rows // 2
    grid_spec = pltpu.PrefetchScalarGridSpec(
        num_scalar_prefetch=1,
        grid=(nseq, nch),
        in_specs=([pl.BlockSpec((1, rows, kw), lambda n, c, pt: (n, 0, 0))]
                  + [page_spec(i) for i in range(ppc)] + [page_spec(i) for i in range(ppc)]
                  + [seq_spec, seq_spec, vec, vec, vec, vec,
                     pl.BlockSpec((1, VAL_DIM), lambda n, c, pt: (0, 0))]),
        out_specs=pl.BlockSpec((1, out_rows, VAL_DIM), lambda n, c, pt: (n, 0, 0)),
        scratch_shapes=[
            pltpu.VMEM((rows, 1), F32),
            pltpu.VMEM((rows, 1), F32),
            pltpu.VMEM((rows, kw), F32),
        ],
    )
    return pl.pallas_call(
        kern,
        grid_spec=grid_spec,
        out_shape=jax.ShapeDtypeStruct((nseq, out_rows, VAL_DIM), F32),
        compiler_params=_cparams(("arbitrary", "arbitrary")),
        name="flash_sample",
    )(page_table, qblk, *([cache_k4] * ppc), *([cache_v4] * ppc), knew, vnew, *lams, sg)


def _attn_out_kernel(o_ref, gs_ref, w_ref, x_ref, *rest, final):
    if final:
        fg_ref, y_ref = rest
    else:
        (y_ref,) = rest
    y = jnp.dot(o_ref[...] * gs_ref[...], w_ref[...], preferred_element_type=F32) + x_ref[...]
    if final:
        y = _rms(y, fg_ref[...])
    y_ref[...] = y


def _attn_out(o, gs, w, x, fg, *, tm):
    n, d = x.shape
    e = w.shape[0]
    final = fg is not None
    row = lambda wd: pl.BlockSpec((tm, wd), lambda i: (i, 0))
    in_specs = [row(e), row(e), _const_spec(w.shape), row(d)]
    args = [o, gs, w, x]
    if final:
        in_specs.append(_const_spec((1, d)))
        args.append(fg)
    return pl.pallas_call(
        functools.partial(_attn_out_kernel, final=final),
        grid=(n // tm,),
        in_specs=in_specs,
        out_specs=row(d),
        out_shape=jax.ShapeDtypeStruct((n, d), F32),
        compiler_params=_cparams(("arbitrary",)),
        name="attn_out",
    )(*args)


def _final_norm_kernel(x_ref, g_ref, y_ref):
    y_ref[...] = _rms(x_ref[...], g_ref[...])


def _final_norm(x, g, *, tm):
    n, d = x.shape
    row = pl.BlockSpec((tm, d), lambda i: (i, 0))
    return pl.pallas_call(
        _final_norm_kernel, grid=(n // tm,), in_specs=[row, _const_spec((1, d))], out_specs=row,
        out_shape=jax.ShapeDtypeStruct((n, d), F32), compiler_params=_cparams(("arbitrary",)),
        name="final_norm",
    )(x, g)


def _rope_tables(pos):
    half = ROT_DIM // 2
    inv_freq = ROPE_THETA ** (-jnp.arange(half, dtype=F32) / half)
    ang = pos.astype(F32)[:, None] * inv_freq[None, :]
    cos, sin = jnp.cos(ang), jnp.sin(ang)
    n = pos.shape[0]
    pad = jnp.zeros((n, HEAD_DIM - ROT_DIM), F32)
    zero = jnp.zeros((n, half), F32)
    c64 = jnp.concatenate([cos, cos, pad + 1.0], axis=1)
    up64 = jnp.concatenate([-sin, zero, pad], axis=1)
    dn64 = jnp.concatenate([zero, sin, pad], axis=1)
    rep = lambda t: jnp.concatenate([t, t], axis=1)
    return rep(c64), rep(up64), rep(dn64)


def _pick_tile(n, pref):
    t = min(n, pref)
    while n % t:
        t //= 2
    return t


def kernel(x_prompt, x_sample, state_conv, cache_k, cache_v, page_table, norm_g, final_norm_g,
           conv_w_in, conv_dw_w, conv_dw_b, conv_ln_g, conv_ln_b, conv_w_out,
           attn_w_in, attn_lambda_q1, attn_lambda_k1, attn_lambda_q2, attn_lambda_k2,
           attn_subln_g, attn_w_out):
    batch, seq, d = x_prompt.shape
    nseq, ntok, _ = x_sample.shape
    depth = norm_g.shape[0]
    e = conv_w_out.shape[1]
    n_heads = e // VAL_DIM
    kvh = n_heads // GROUP
    qw = n_heads * 2 * HEAD_DIM
    kvw = kvh * 2 * HEAD_DIM
    n_chunks = e // LANES
    hist = CONV_WIDTH - 1
    n_pages = page_table.shape[1]
    past = n_pages * PAGE_SIZE

    tm_conv = _pick_tile(seq, 256)
    tm_proj = _pick_tile(seq, 512)
    tq = _pick_tile(seq, 512)
    ct = 512
    ppc = _pick_tile(n_pages, 8)

    yp = x_prompt.reshape(batch * seq, d)
    ys = jnp.swapaxes(x_sample, 0, 1).reshape(ntok * nseq, d)
    ns = ntok * nseq

    tabs_p = _rope_tables(jnp.arange(seq, dtype=jnp.int32))
    tabs_s = _rope_tables(past + jnp.repeat(jnp.arange(ntok, dtype=jnp.int32), nseq))

    cache_k4 = cache_k.reshape(cache_k.shape[0], cache_k.shape[1], PAGE_SIZE * kvh, 2 * HEAD_DIM)
    cache_v4 = cache_v.reshape(cache_v.shape[0], cache_v.shape[1], PAGE_SIZE * kvh, VAL_DIM)
    eye_h = jnp.eye(kvh, dtype=BF16)
    eye_c = jnp.eye(2, dtype=BF16)

    conv_p, conv_s, k_p, v_p, k_s, v_s = [], [], [], [], [], []
    for i in range(depth):
        j = i // N_MIXERS
        ng = norm_g[i].reshape(1, d)
        last = i == depth - 1
        if i % N_MIXERS == 0:
            win = conv_w_in[j].astype(BF16)
            wout = conv_w_out[j].astype(BF16)
            dww3 = jnp.pad(conv_dw_w[j], ((0, HALO - CONV_WIDTH), (0, 0))).reshape(HALO, n_chunks, LANES)
            dww3 = jnp.swapaxes(dww3, 0, 1)
            dwb3 = conv_dw_b[j].reshape(n_chunks, 1, LANES)
            lng = conv_ln_g[j].reshape(1, e)
            lnb = conv_ln_b[j].reshape(1, e)
            yp, sp = _conv_prompt(yp, ng, win, dww3, dwb3, lng, lnb, wout, batch=batch, seq=seq, tm=tm_conv)
            prev_t = jnp.swapaxes(state_conv[j], 0, 1)
            ys, ss_t = _conv_sample(ys, ng, win, prev_t, dww3, dwb3, lng, lnb, wout, nseq=nseq, ntok=ntok)
            conv_p.append(sp)
            conv_s.append(jnp.swapaxes(ss_t, 0, 1))
            if last:
                yp = _final_norm(yp, final_norm_g.reshape(1, d), tm=tm_proj)
                ys = _final_norm(ys, final_norm_g.reshape(1, d), tm=ns)
        else:
            lam_init = 0.8 - 0.6 * math.exp(-0.3 * i)
            win = attn_w_in[j].astype(BF16)
            wout = attn_w_out[j].astype(BF16)
            lams = [a[j].reshape(1, HEAD_DIM) for a in
                    (attn_lambda_q1, attn_lambda_k1, attn_lambda_q2, attn_lambda_k2)]
            sg = attn_subln_g[j].reshape(1, VAL_DIM)
            fg = final_norm_g.reshape(1, d) if last else None

            q, kf, kb, vf, vt5, gs = _attn_proj(yp, ng, win, *tabs_p, qw=qw, kvw=kvw, tm=tm_proj,
                                                seq_tiles=seq // tm_proj, transpose_v=True)
            o = _flash_prompt(q, kb, vt5, lams, sg, batch=batch, seq=seq, tq=tq, ct=ct, lam_init=lam_init)
            yp = _attn_out(o, gs, wout, yp, fg, tm=tm_proj)
            k_p.append(kf.reshape(batch, seq, kvh, 2 * HEAD_DIM))
            v_p.append(vf.reshape(batch, seq, kvh, VAL_DIM))

            q, kf, kb, vf, vb, gs = _attn_proj(ys, ng, win, *tabs_s, qw=qw, kvw=kvw, tm=ns,
                                               seq_tiles=1, transpose_v=False)
            q6 = q.reshape(ntok, nseq, kvh, GROUP, 2, HEAD_DIM).transpose(1, 2, 4, 3, 0, 5)
            q6 = q6.reshape(nseq, kvh, 2, GROUP * ntok, HEAD_DIM)
            qblk = (q6[:, :, :, :, None, None, :] * eye_h[None, :, None, None, :, None, None]
                    * eye_c[None, None, :, None, None, :, None])
            qblk = qblk.reshape(nseq, kvh * 2 * GROUP * ntok, kvw)
            pad_new = lambda a: jnp.pad(jnp.swapaxes(a.reshape(ntok, nseq, kvw), 0, 1),
                                        ((0, 0), (0, PAGE_SIZE - ntok), (0, 0)))
            od = _flash_sample(page_table, qblk, cache_k4, cache_v4, j, pad_new(kb), pad_new(vb), lams, sg,
                               ppc=ppc, ntok=ntok, lam_init=lam_init)
            o = od.reshape(nseq, kvh, GROUP, ntok, VAL_DIM).transpose(3, 0, 1, 2, 4).reshape(ns, e)
            ys = _attn_out(o.astype(BF16), gs, wout, ys, fg, tm=ns)
            to_rows = lambda a: jnp.swapaxes(a.reshape(ntok, nseq, kvh, VAL_DIM), 0, 1)
            k_s.append(to_rows(kf))
            v_s.append(to_rows(vf))

    y_prompt = yp.reshape(batch, seq, d)
    y_sample = jnp.swapaxes(ys.reshape(ntok, nseq, d), 0, 1)
    return (y_prompt, y_sample, jnp.stack(conv_p), jnp.stack(conv_s),
            jnp.stack(k_p), jnp.stack(v_p), jnp.stack(k_s), jnp.stack(v_s))
```

```python
import functools
import math

import jax
import jax.numpy as jnp
from jax import lax
from jax.experimental import pallas as pl
from jax.experimental.pallas import tpu as pltpu

F32 = jnp.float32
BF16 = jnp.bfloat16

N_MIXERS = 2
CONV_WIDTH = 31
HEAD_DIM = 64
VAL_DIM = 2 * HEAD_DIM
GROUP = 4
ROT_DIM = HEAD_DIM // 4
ROPE_THETA = 500000.0
RMS_EPS = 1e-6
LN_EPS = 1e-5
SCALE = HEAD_DIM ** -0.5
Q_SCALE = SCALE * math.log2(math.e)
PAGE_SIZE = 128

LANES = 128
V_ROWS = VAL_DIM + 16
HALO = 32
VMEM_LIMIT = 56 * 1024 * 1024


def _cparams(sem):
    return pltpu.CompilerParams(dimension_semantics=sem, vmem_limit_bytes=VMEM_LIMIT)


def _const_spec(shape, single=False):
    nd = len(shape)
    return pl.BlockSpec(shape, lambda *_: (0,) * nd, pipeline_mode=pl.Buffered(1) if single else None)


def _rms(x, g):
    return x * lax.rsqrt(jnp.mean(x * x, axis=-1, keepdims=True) + RMS_EPS) * g


def _sigmoid(x):
    return 1.0 / (1.0 + jnp.exp(-x))


def _conv_tail(cb_ref, m_ref, gate, lng_ref, lnb_ref, n_chunks):
    tot = cb_ref[0]
    for c in range(1, n_chunks):
        tot = tot + cb_ref[c]
    width = n_chunks * LANES
    mu = jnp.sum(tot, axis=-1, keepdims=True) / width
    sq = None
    for c in range(n_chunks):
        d = cb_ref[c] - mu
        sq = d * d if sq is None else sq + d * d
    rstd = lax.rsqrt(jnp.sum(sq, axis=-1, keepdims=True) / width + LN_EPS)
    for c in range(n_chunks):
        sl = slice(c * LANES, (c + 1) * LANES)
        cn = (cb_ref[c] - mu) * rstd * lng_ref[:, sl] + lnb_ref[:, sl]
        act = cn * _sigmoid(cn)
        g = gate[:, sl]
        m_ref[:, sl] = (act * (g * _sigmoid(g))).astype(BF16)


def _conv_prompt_kernel(x_ref, ng_ref, win_ref, dww_ref, dwb_ref, lng_ref, lnb_ref, wout_ref,
                        y_ref, st_ref, upad_ref, cb_ref, m_ref, *, tm, nt, e):
    t = pl.program_id(1)
    n_chunks = e // LANES

    @pl.when(t == 0)
    def _():
        upad_ref[:, 0:HALO, :] = jnp.zeros((n_chunks, HALO, LANES), F32)

    x = x_ref[...]
    h = _rms(x, ng_ref[...]).astype(BF16)
    a = jnp.dot(h, win_ref[:, 0:e], preferred_element_type=F32)
    b = jnp.dot(h, win_ref[:, e:2 * e], preferred_element_type=F32)
    u = a * _sigmoid(b)
    for c in range(n_chunks):
        upad_ref[c, HALO:HALO + tm, :] = u[:, c * LANES:(c + 1) * LANES]

    off = HALO - (CONV_WIDTH - 1)

    def chunk(c, carry):
        acc = jnp.broadcast_to(dwb_ref[c], (tm, LANES))
        for k in range(CONV_WIDTH):
            acc = acc + dww_ref[c, k:k + 1, :] * upad_ref[c, off + k:off + k + tm, :]
        cb_ref[c] = acc
        return carry

    lax.fori_loop(0, n_chunks, chunk, 0)

    gate = jnp.dot(h, win_ref[:, 2 * e:3 * e], preferred_element_type=F32)
    _conv_tail(cb_ref, m_ref, gate, lng_ref, lnb_ref, n_chunks)
    y_ref[...] = jnp.dot(m_ref[...], wout_ref[...], preferred_element_type=F32) + x

    @pl.when(t == nt - 1)
    def _():
        for c in range(n_chunks):
            st_ref[0, :, c * LANES:(c + 1) * LANES] = upad_ref[c, tm + off:tm + HALO, :]

    upad_ref[:, 0:HALO, :] = upad_ref[:, tm:tm + HALO, :]


def _conv_prompt(x, ng, win, dww3, dwb3, lng, lnb, wout, *, batch, seq, tm):
    n, d = x.shape
    e = wout.shape[0]
    nt = seq // tm
    n_chunks = e // LANES
    kern = functools.partial(_conv_prompt_kernel, tm=tm, nt=nt, e=e)
    return pl.pallas_call(
        kern,
        grid=(batch, nt),
        in_specs=[
            pl.BlockSpec((tm, d), lambda b, t: (b * nt + t, 0)),
            _const_spec((1, d)),
            _const_spec(win.shape, single=True),
            _const_spec(dww3.shape),
            _const_spec(dwb3.shape),
            _const_spec((1, e)),
            _const_spec((1, e)),
            _const_spec(wout.shape, single=True),
        ],
        out_specs=[
            pl.BlockSpec((tm, d), lambda b, t: (b * nt + t, 0)),
            pl.BlockSpec((1, CONV_WIDTH - 1, e), lambda b, t: (b, 0, 0)),
        ],
        out_shape=[
            jax.ShapeDtypeStruct((n, d), F32),
            jax.ShapeDtypeStruct((batch, CONV_WIDTH - 1, e), F32),
        ],
        scratch_shapes=[
            pltpu.VMEM((n_chunks, HALO + tm, LANES), F32),
            pltpu.VMEM((n_chunks, tm, LANES), F32),
            pltpu.VMEM((tm, e), BF16),
        ],
        compiler_params=_cparams(("arbitrary", "arbitrary")),
        name="conv_prompt",
    )(x, ng, win, dww3, dwb3, lng, lnb, wout)


def _conv_sample_kernel(x_ref, ng_ref, win_ref, prev_ref, dww_ref, dwb_ref, lng_ref, lnb_ref, wout_ref,
                        y_ref, st_ref, u_ref, cb_ref, m_ref, *, nseq, ntok, e):
    n_chunks = e // LANES
    hist = CONV_WIDTH - 1
    x = x_ref[...]
    h = _rms(x, ng_ref[...]).astype(BF16)
    a = jnp.dot(h, win_ref[:, 0:e], preferred_element_type=F32)
    b = jnp.dot(h, win_ref[:, e:2 * e], preferred_element_type=F32)
    u_ref[...] = a * _sigmoid(b)

    def row(r, sl):
        if r < hist:
            return prev_ref[r, :, sl]
        return u_ref[(r - hist) * nseq:(r - hist + 1) * nseq, sl]

    for c in range(n_chunks):
        sl = slice(c * LANES, (c + 1) * LANES)
        for t in range(ntok):
            acc = jnp.broadcast_to(dwb_ref[c], (nseq, LANES))
            for k in range(CONV_WIDTH):
                acc = acc + dww_ref[c, k:k + 1, :] * row(t + k, sl)
            cb_ref[c, t * nseq:(t + 1) * nseq, :] = acc
        for r in range(hist):
            st_ref[r, :, sl] = row(r + ntok, sl)

    gate = jnp.dot(h, win_ref[:, 2 * e:3 * e], preferred_element_type=F32)
    _conv_tail(cb_ref, m_ref, gate, lng_ref, lnb_ref, n_chunks)
    y_ref[...] = jnp.dot(m_ref[...], wout_ref[...], preferred_element_type=F32) + x


def _conv_sample(x, ng, win, prev_t, dww3, dwb3, lng, lnb, wout, *, nseq, ntok):
    n, d = x.shape
    e = wout.shape[0]
    n_chunks = e // LANES
    kern = functools.partial(_conv_sample_kernel, nseq=nseq, ntok=ntok, e=e)
    args = (x, ng, win, prev_t, dww3, dwb3, lng, lnb, wout)
    return pl.pallas_call(
        kern,
        grid=(1,),
        in_specs=[_const_spec(a.shape) for a in args],
        out_specs=[_const_spec((n, d)), _const_spec(prev_t.shape)],
        out_shape=[jax.ShapeDtypeStruct((n, d), F32), jax.ShapeDtypeStruct(prev_t.shape, F32)],
        scratch_shapes=[
            pltpu.VMEM((n, e), F32),
            pltpu.VMEM((n_chunks, n, LANES), F32),
            pltpu.VMEM((n, e), BF16),
        ],
        compiler_params=_cparams(("arbitrary",)),
        name="conv_sample",
    )(*args)


def _rope_tile(x, cos, sin_up, sin_dn):
    return (x * cos + pltpu.roll(x, LANES - ROT_DIM // 2, axis=1) * sin_up
            + pltpu.roll(x, ROT_DIM // 2, axis=1) * sin_dn)


def _attn_proj_kernel(x_ref, ng_ref, win_ref, cos_ref, sup_ref, sdn_ref, *rest, qw, kvw, transpose_v, aliased):
    q_ref, kf_ref, kb_ref, vf_ref, vb_ref, gs_ref = rest[2:] if aliased else rest
    kvh = kvw // LANES
    tm = x_ref.shape[0]
    x = x_ref[...]
    h = _rms(x, ng_ref[...]).astype(BF16)
    cos, sup, sdn = cos_ref[...], sup_ref[...], sdn_ref[...]
    q = jnp.dot(h, win_ref[:, 0:qw], preferred_element_type=F32)
    for j in range(qw // LANES):
        sl = slice(j * LANES, (j + 1) * LANES)
        q_ref[:, sl] = (_rope_tile(q[:, sl], cos, sup, sdn) * Q_SCALE).astype(BF16)
    k = jnp.dot(h, win_ref[:, qw:qw + kvw], preferred_element_type=F32)
    for j in range(kvh):
        sl = slice(j * LANES, (j + 1) * LANES)
        kr = _rope_tile(k[:, sl], cos, sup, sdn)
        kf_ref[0, pl.ds(j, tm, stride=kvh), :] = kr
        kb_ref[:, sl] = kr.astype(BF16)
    v = jnp.dot(h, win_ref[:, qw + kvw:qw + 2 * kvw], preferred_element_type=F32)
    for j in range(kvh):
        vf_ref[0, pl.ds(j, tm, stride=kvh), :] = v[:, j * LANES:(j + 1) * LANES]
    if transpose_v:
        vt = v.T.astype(BF16)
        for j in range(kvh):
            vb_ref[0, 0, j, 0:VAL_DIM, :] = vt[j * LANES:(j + 1) * LANES, :]
            vb_ref[0, 0, j, VAL_DIM:V_ROWS, :] = jnp.ones((V_ROWS - VAL_DIM, tm), BF16)
    else:
        vb_ref[...] = v.astype(BF16)
    g = jnp.dot(h, win_ref[:, qw + 2 * kvw:], preferred_element_type=F32)
    gs_ref[...] = (g * _sigmoid(g)).astype(BF16)


def _attn_proj(x, ng, win, cos, sup, sdn, kv_rows, *, layer, n_layers, qw, kvw, tm, seq_tiles, transpose_v):
    n, d = x.shape
    nt = n // tm
    kvh = kvw // LANES
    gw = win.shape[1] - qw - 2 * kvw
    row = lambda w: pl.BlockSpec((tm, w), lambda i: (i, 0))
    tab = pl.BlockSpec((tm, LANES), lambda i: (i % seq_tiles, 0))
    if transpose_v:
        vb_spec = pl.BlockSpec((1, 1, kvh, V_ROWS, tm), lambda i: (i // seq_tiles, i % seq_tiles, 0, 0, 0))
        vb_shape = jax.ShapeDtypeStruct((nt // seq_tiles, seq_tiles, kvh, V_ROWS, tm), BF16)
    else:
        vb_spec = row(kvw)
        vb_shape = jax.ShapeDtypeStruct((n, kvw), BF16)
    aliased = kv_rows is not None
    kern = functools.partial(_attn_proj_kernel, qw=qw, kvw=kvw, transpose_v=transpose_v, aliased=aliased)
    rows_spec = pl.BlockSpec((1, tm * kvh, LANES), lambda i: (layer, i, 0))
    rows_shape = jax.ShapeDtypeStruct((n_layers, n * kvh, LANES), F32)
    in_specs = [row(d), _const_spec((1, d)), _const_spec(win.shape), tab, tab, tab]
    args = [x, ng, win, cos, sup, sdn]
    if aliased:
        in_specs += [pl.BlockSpec(memory_space=pl.ANY)] * 2
        args += list(kv_rows)
    return pl.pallas_call(
        kern,
        grid=(nt,),
        in_specs=in_specs,
        out_specs=[row(qw), rows_spec, row(kvw), rows_spec, vb_spec, row(gw)],
        out_shape=[
            jax.ShapeDtypeStruct((n, qw), BF16),
            rows_shape,
            jax.ShapeDtypeStruct((n, kvw), BF16),
            rows_shape,
            vb_shape,
            jax.ShapeDtypeStruct((n, gw), BF16),
        ],
        input_output_aliases={6: 1, 7: 3} if aliased else {},
        compiler_params=_cparams(("arbitrary",)),
        name="attn_proj",
    )(*args)


def _diff_lambda(lq1_ref, lk1_ref, lq2_ref, lk2_ref, lam_init):
    s1 = jnp.sum(lq1_ref[...] * lk1_ref[...], axis=-1, keepdims=True)
    s2 = jnp.sum(lq2_ref[...] * lk2_ref[...], axis=-1, keepdims=True)
    return jnp.exp(s1) - jnp.exp(s2) + lam_init


def _flash_prompt_kernel(q_ref, k_ref, vt_ref, lq1_ref, lk1_ref, lq2_ref, lk2_ref, sg_ref, o_ref,
                         qp_ref, m_ref, l_ref, acc_ref, s0_ref, s1_ref, x0_ref, x1_ref,
                         *, tq, sb, ct, lam_init):
    qi = pl.program_id(2)
    r = 2 * GROUP * tq
    lane = lax.broadcasted_iota(jnp.int32, (tq, LANES), 1)
    zero = jnp.zeros((tq, LANES), BF16)
    for g in range(GROUP):
        qg = q_ref[:, g * LANES:(g + 1) * LANES]
        qp_ref[(2 * g) * tq:(2 * g + 1) * tq, :] = jnp.where(lane < HEAD_DIM, qg, zero)
        qp_ref[(2 * g + 1) * tq:(2 * g + 2) * tq, :] = jnp.where(lane >= HEAD_DIM, qg, zero)
    m_ref[...] = jnp.full((1, r), -jnp.inf, F32)
    l_ref[...] = jnp.zeros((1, r), F32)
    acc_ref[...] = jnp.zeros((VAL_DIM, r), F32)

    n_ct = r // ct
    nt_dims = (((1,), (1,)), ((), ()))

    def scores_into(s_ref, x_ref, j):
        kb = k_ref[pl.ds(pl.multiple_of(j * sb, sb), sb), :]
        for c in range(n_ct):
            cs = slice(c * ct, (c + 1) * ct)
            s = lax.dot_general(kb, qp_ref[cs, :], nt_dims, preferred_element_type=F32)
            s_ref[:, cs] = s
            x_ref[:, cs] = jnp.max(s, axis=0, keepdims=True)

    def absorb(s_ref, x_ref, j, masked):
        vt = vt_ref[0, j, 0]
        for c in range(n_ct):
            cs = slice(c * ct, (c + 1) * ct)
            s = s_ref[:, cs]
            if masked:
                key = j * sb + lax.broadcasted_iota(jnp.int32, (sb, ct), 0)
                tok = qi * tq + lax.rem(c * ct + lax.broadcasted_iota(jnp.int32, (sb, ct), 1), tq)
                s = jnp.where(key <= tok, s, -jnp.inf)
                mx = jnp.max(s, axis=0, keepdims=True)
            else:
                mx = x_ref[:, cs]
            m_old = m_ref[:, cs]
            m_new = jnp.maximum(m_old, mx)
            alpha = jnp.exp2(m_old - m_new)
            p = jnp.exp2(s - m_new)
            pv = jnp.dot(vt, p.astype(BF16), preferred_element_type=F32)
            l_ref[:, cs] = alpha * l_ref[:, cs] + pv[VAL_DIM:VAL_DIM + 1, :]
            acc_ref[:, cs] = alpha * acc_ref[:, cs] + pv[0:VAL_DIM, :]
            m_ref[:, cs] = m_new

    n_full = (qi * tq) // sb
    scores_into(s0_ref, x0_ref, 0)

    def pair(jj, carry):
        j0 = 2 * jj
        scores_into(s1_ref, x1_ref, j0 + 1)
        absorb(s0_ref, x0_ref, j0, False)
        scores_into(s0_ref, x0_ref, j0 + 2)
        absorb(s1_ref, x1_ref, j0 + 1, False)
        return carry

    lax.fori_loop(0, n_full // 2, pair, 0)
    odd = lax.rem(n_full, 2) == 1

    @pl.when(odd)
    def _():
        scores_into(s1_ref, x1_ref, n_full)
        absorb(s0_ref, x0_ref, n_full - 1, False)
        absorb(s1_ref, x1_ref, n_full, True)

    @pl.when(jnp.logical_not(odd))
    def _():
        absorb(s0_ref, x0_ref, n_full, True)

    lam = _diff_lambda(lq1_ref, lk1_ref, lq2_ref, lk2_ref, lam_init)
    for g in range(GROUP):
        s1 = slice((2 * g) * tq, (2 * g + 1) * tq)
        s2 = slice((2 * g + 1) * tq, (2 * g + 2) * tq)
        o1 = acc_ref[:, s1] * (1.0 / l_ref[:, s1])
        o2 = acc_ref[:, s2] * (1.0 / l_ref[:, s2])
        d = o1 - lam * o2
        dn = d * lax.rsqrt(jnp.mean(d * d, axis=0, keepdims=True) + RMS_EPS)
        o_ref[:, g * LANES:(g + 1) * LANES] = ((dn.T * sg_ref[...]) * (1.0 - lam_init)).astype(BF16)


def _flash_prompt(q, kb, vt5, lams, sg, *, batch, seq, tq, ct, lam_init):
    n, qw = q.shape
    kvh = kb.shape[1] // LANES
    sb = vt5.shape[-1]
    nq = seq // tq
    r = 2 * GROUP * tq
    kern = functools.partial(_flash_prompt_kernel, tq=tq, sb=sb, ct=ct, lam_init=lam_init)
    vec = _const_spec((1, HEAD_DIM))
    return pl.pallas_call(
        kern,
        grid=(batch, kvh, nq),
        in_specs=[
            pl.BlockSpec((tq, GROUP * LANES), lambda b, h, i: (b * nq + i, h)),
            pl.BlockSpec((seq, LANES), lambda b, h, i: (b, h)),
            pl.BlockSpec((1, seq // sb, 1, V_ROWS, sb), lambda b, h, i: (b, 0, h, 0, 0)),
            vec, vec, vec, vec,
            _const_spec((1, VAL_DIM)),
        ],
        out_specs=pl.BlockSpec((tq, GROUP * LANES), lambda b, h, i: (b * nq + i, h)),
        out_shape=jax.ShapeDtypeStruct((n, qw), BF16),
        scratch_shapes=[
            pltpu.VMEM((r, LANES), BF16),
            pltpu.VMEM((1, r), F32),
            pltpu.VMEM((1, r), F32),
            pltpu.VMEM((VAL_DIM, r), F32),
            pltpu.VMEM((sb, r), F32),
            pltpu.VMEM((sb, r), F32),
            pltpu.VMEM((1, r), F32),
            pltpu.VMEM((1, r), F32),
        ],
        compiler_params=_cparams(("arbitrary", "arbitrary", "arbitrary")),
        name="flash_prompt",
    )(q, kb, vt5, *lams, sg)


def _flash_sample_kernel(pt_ref, q_ref, *refs, ppc, ntok, lam_init):
    del pt_ref
    k_refs = refs[0:ppc]
    v_refs = refs[ppc:2 * ppc]
    kn_ref, vn_ref, lq1_ref, lk1_ref, lq2_ref, lk2_ref, sg_ref, o_ref, m_ref, l_ref, acc_ref = refs[2 * ppc:]
    c = pl.program_id(1)
    rows = q_ref.shape[1]
    kvh = q_ref.shape[2] // LANES
    rpk = rows // kvh

    @pl.when(c == 0)
    def _():
        m_ref[...] = jnp.full(m_ref.shape, -jnp.inf, F32)
        l_ref[...] = jnp.zeros(l_ref.shape, F32)
        acc_ref[...] = jnp.zeros(acc_ref.shape, F32)

    q = q_ref[0]

    def update(s_list, v_list):
        s = jnp.concatenate(s_list, axis=1) if len(s_list) > 1 else s_list[0]
        m_old = m_ref[...]
        m_new = jnp.maximum(m_old, jnp.max(s, axis=1, keepdims=True))
        alpha = jnp.exp2(m_old - m_new)
        p = jnp.exp2(s - m_new)
        l_ref[...] = alpha * l_ref[...] + jnp.sum(p, axis=1, keepdims=True)
        pb = p.astype(BF16)
        pv = None
        for i, v in enumerate(v_list):
            d = jnp.dot(pb[:, i * PAGE_SIZE:(i + 1) * PAGE_SIZE], v, preferred_element_type=F32)
            pv = d if pv is None else pv + d
        acc_ref[...] = alpha * acc_ref[...] + pv
        m_ref[...] = m_new

    def page(ref):
        heads = [ref[0, 0, pl.ds(h, PAGE_SIZE, stride=kvh), :] for h in range(kvh)]
        return jnp.concatenate(heads, axis=1).astype(BF16)

    nt_dims = (((1,), (1,)), ((), ()))
    s_list = [lax.dot_general(q, page(k_refs[i]), nt_dims, preferred_element_type=F32) for i in range(ppc)]
    update(s_list, [page(v_refs[i]) for i in range(ppc)])

    @pl.when(c == pl.num_programs(1) - 1)
    def _():
        s = lax.dot_general(q, kn_ref[0], nt_dims, preferred_element_type=F32)
        key = lax.broadcasted_iota(jnp.int32, s.shape, 1)
        tok = lax.rem(lax.broadcasted_iota(jnp.int32, s.shape, 0), ntok)
        update([jnp.where(key <= tok, s, -jnp.inf)], [vn_ref[0]])
        lam = _diff_lambda(lq1_ref, lk1_ref, lq2_ref, lk2_ref, lam_init)
        half = rpk // 2
        for h in range(kvh):
            blk = (acc_ref[h * rpk:(h + 1) * rpk, h * LANES:(h + 1) * LANES]
                   / l_ref[h * rpk:(h + 1) * rpk, :])
            d = blk[0:half] - lam * blk[half:rpk]
            dn = d * lax.rsqrt(jnp.mean(d * d, axis=-1, keepdims=True) + RMS_EPS)
            o_ref[0, h * half:(h + 1) * half, :] = (dn * sg_ref[...]) * (1.0 - lam_init)


def _flash_sample(page_table, qblk, cache_k4, cache_v4, layer, knew, vnew, lams, sg, *, ppc, ntok, lam_init):
    nseq, rows, kw = qblk.shape
    n_pages = page_table.shape[1]
    nch = n_pages // ppc
    kvh = kw // LANES
    kern = functools.partial(_flash_sample_kernel, ppc=ppc, ntok=ntok, lam_init=lam_init)

    def page_spec(i):
        return pl.BlockSpec((1, 1, PAGE_SIZE * kvh, LANES), lambda n, c, pt: (layer, pt[n, c * ppc + i], 0, 0))

    seq_spec = pl.BlockSpec((1, PAGE_SIZE, kw), lambda n, c, pt: (n, 0, 0))
    vec = pl.BlockSpec((1, HEAD_DIM), lambda n, c, pt: (0, 0))
    out_rows = rows // 2
    grid_spec = pltpu.PrefetchScalarGridSpec(
        num_scalar_prefetch=1,
        grid=(nseq, nch),
        in_specs=([pl.BlockSpec((1, rows, kw), lambda n, c, pt: (n, 0, 0))]
                  + [page_spec(i) for i in range(ppc)] + [page_spec(i) for i in range(ppc)]
                  + [seq_spec, seq_spec, vec, vec, vec, vec,
                     pl.BlockSpec((1, VAL_DIM), lambda n, c, pt: (0, 0))]),
        out_specs=pl.BlockSpec((1, out_rows, VAL_DIM), lambda n, c, pt: (n, 0, 0)),
        scratch_shapes=[
            pltpu.VMEM((rows, 1), F32),
            pltpu.VMEM((rows, 1), F32),
            pltpu.VMEM((rows, kw), F32),
        ],
    )
    return pl.pallas_call(
        kern,
        grid_spec=grid_spec,
        out_shape=jax.ShapeDtypeStruct((nseq, out_rows, VAL_DIM), F32),
        compiler_params=_cparams(("arbitrary", "arbitrary")),
        name="flash_sample",
    )(page_table, qblk, *([cache_k4] * ppc), *([cache_v4] * ppc), knew, vnew, *lams, sg)


def _attn_out_kernel(o_ref, gs_ref, w_ref, x_ref, *rest, final):
    if final:
        fg_ref, y_ref = rest
    else:
        (y_ref,) = rest
    y = jnp.dot(o_ref[...] * gs_ref[...], w_ref[...], preferred_element_type=F32) + x_ref[...]
    if final:
        y = _rms(y, fg_ref[...])
    y_ref[...] = y


def _attn_out(o, gs, w, x, fg, *, tm):
    n, d = x.shape
    e = w.shape[0]
    final = fg is not None
    row = lambda wd: pl.BlockSpec((tm, wd), lambda i: (i, 0))
    in_specs = [row(e), row(e), _const_spec(w.shape), row(d)]
    args = [o, gs, w, x]
    if final:
        in_specs.append(_const_spec((1, d)))
        args.append(fg)
    return pl.pallas_call(
        functools.partial(_attn_out_kernel, final=final),
        grid=(n // tm,),
        in_specs=in_specs,
        out_specs=row(d),
        out_shape=jax.ShapeDtypeStruct((n, d), F32),
        compiler_params=_cparams(("arbitrary",)),
        name="attn_out",
    )(*args)


def _final_norm_kernel(x_ref, g_ref, y_ref):
    y_ref[...] = _rms(x_ref[...], g_ref[...])


def _final_norm(x, g, *, tm):
    n, d = x.shape
    row = pl.BlockSpec((tm, d), lambda i: (i, 0))
    return pl.pallas_call(
        _final_norm_kernel, grid=(n // tm,), in_specs=[row, _const_spec((1, d))], out_specs=row,
        out_shape=jax.ShapeDtypeStruct((n, d), F32), compiler_params=_cparams(("arbitrary",)),
        name="final_norm",
    )(x, g)


def _rope_tables(pos):
    half = ROT_DIM // 2
    inv_freq = ROPE_THETA ** (-jnp.arange(half, dtype=F32) / half)
    ang = pos.astype(F32)[:, None] * inv_freq[None, :]
    cos, sin = jnp.cos(ang), jnp.sin(ang)
    n = pos.shape[0]
    pad = jnp.zeros((n, HEAD_DIM - ROT_DIM), F32)
    zero = jnp.zeros((n, half), F32)
    c64 = jnp.concatenate([cos, cos, pad + 1.0], axis=1)
    up64 = jnp.concatenate([-sin, zero, pad], axis=1)
    dn64 = jnp.concatenate([zero, sin, pad], axis=1)
    rep = lambda t: jnp.concatenate([t, t], axis=1)
    return rep(c64), rep(up64), rep(dn64)


def _pick_tile(n, pref):
    t = min(n, pref)
    while n % t:
        t //= 2
    return t


def kernel(x_prompt, x_sample, state_conv, cache_k, cache_v, page_table, norm_g, final_norm_g,
           conv_w_in, conv_dw_w, conv_dw_b, conv_ln_g, conv_ln_b, conv_w_out,
           attn_w_in, attn_lambda_q1, attn_lambda_k1, attn_lambda_q2, attn_lambda_k2,
           attn_subln_g, attn_w_out):
    batch, seq, d = x_prompt.shape
    nseq, ntok, _ = x_sample.shape
    depth = norm_g.shape[0]
    e = conv_w_out.shape[1]
    n_heads = e // VAL_DIM
    kvh = n_heads // GROUP
    qw = n_heads * 2 * HEAD_DIM
    kvw = kvh * 2 * HEAD_DIM
    n_chunks = e // LANES
    n_pages = page_table.shape[1]
    past = n_pages * PAGE_SIZE

    tm_conv = _pick_tile(seq, 512)
    tm_proj = _pick_tile(seq, 512)
    tq = _pick_tile(seq, 512)
    ct = 512
    ppc = _pick_tile(n_pages, 16)

    yp = x_prompt.reshape(batch * seq, d)
    ys = jnp.swapaxes(x_sample, 0, 1).reshape(ntok * nseq, d)
    ns = ntok * nseq

    tabs_p = _rope_tables(jnp.arange(seq, dtype=jnp.int32))
    tabs_s = _rope_tables(past + jnp.repeat(jnp.arange(ntok, dtype=jnp.int32), nseq))

    cache_k4 = cache_k.reshape(cache_k.shape[0], cache_k.shape[1], PAGE_SIZE * kvh, 2 * HEAD_DIM)
    cache_v4 = cache_v.reshape(cache_v.shape[0], cache_v.shape[1], PAGE_SIZE * kvh, VAL_DIM)
    eye_h = jnp.eye(kvh, dtype=BF16)
    eye_c = jnp.eye(2, dtype=BF16)

    n_attn = depth // N_MIXERS
    conv_p, conv_s = [], []
    kv_p = kv_s = None
    for i in range(depth):
        j = i // N_MIXERS
        ng = norm_g[i].reshape(1, d)
        last = i == depth - 1
        if i % N_MIXERS == 0:
            win = conv_w_in[j].astype(BF16)
            wout = conv_w_out[j].astype(BF16)
            dww3 = jnp.pad(conv_dw_w[j], ((0, HALO - CONV_WIDTH), (0, 0))).reshape(HALO, n_chunks, LANES)
            dww3 = jnp.swapaxes(dww3, 0, 1)
            dwb3 = conv_dw_b[j].reshape(n_chunks, 1, LANES)
            lng = conv_ln_g[j].reshape(1, e)
            lnb = conv_ln_b[j].reshape(1, e)
            yp, sp = _conv_prompt(yp, ng, win, dww3, dwb3, lng, lnb, wout, batch=batch, seq=seq, tm=tm_conv)
            prev_t = jnp.swapaxes(state_conv[j], 0, 1)
            ys, ss_t = _conv_sample(ys, ng, win, prev_t, dww3, dwb3, lng, lnb, wout, nseq=nseq, ntok=ntok)
            conv_p.append(sp)
            conv_s.append(jnp.swapaxes(ss_t, 0, 1))
            if last:
                yp = _final_norm(yp, final_norm_g.reshape(1, d), tm=tm_proj)
                ys = _final_norm(ys, final_norm_g.reshape(1, d), tm=ns)
        else:
            lam_init = 0.8 - 0.6 * math.exp(-0.3 * i)
            win = attn_w_in[j].astype(BF16)
            wout = attn_w_out[j].astype(BF16)
            lams = [a[j].reshape(1, HEAD_DIM) for a in
                    (attn_lambda_q1, attn_lambda_k1, attn_lambda_q2, attn_lambda_k2)]
            sg = attn_subln_g[j].reshape(1, VAL_DIM)
            fg = final_norm_g.reshape(1, d) if last else None

            q, kf, kb, vf, vt5, gs = _attn_proj(yp, ng, win, *tabs_p, kv_p, layer=j, n_layers=n_attn, qw=qw,
                                                kvw=kvw, tm=tm_proj, seq_tiles=seq // tm_proj, transpose_v=True)
            kv_p = (kf, vf)
            o = _flash_prompt(q, kb, vt5, lams, sg, batch=batch, seq=seq, tq=tq, ct=ct, lam_init=lam_init)
            yp = _attn_out(o, gs, wout, yp, fg, tm=tm_proj)

            q, kf, kb, vf, vb, gs = _attn_proj(ys, ng, win, *tabs_s, kv_s, layer=j, n_layers=n_attn, qw=qw,
                                               kvw=kvw, tm=ns, seq_tiles=1, transpose_v=False)
            kv_s = (kf, vf)
            q6 = q.reshape(ntok, nseq, kvh, GROUP, 2, HEAD_DIM).transpose(1, 2, 4, 3, 0, 5)
            q6 = q6.reshape(nseq, kvh, 2, GROUP * ntok, HEAD_DIM)
            qblk = (q6[:, :, :, :, None, None, :] * eye_h[None, :, None, None, :, None, None]
                    * eye_c[None, None, :, None, None, :, None])
            qblk = qblk.reshape(nseq, kvh * 2 * GROUP * ntok, kvw)
            pad_new = lambda a: jnp.pad(jnp.swapaxes(a.reshape(ntok, nseq, kvw), 0, 1),
                                        ((0, 0), (0, PAGE_SIZE - ntok), (0, 0)))
            od = _flash_sample(page_table, qblk, cache_k4, cache_v4, j, pad_new(kb), pad_new(vb), lams, sg,
                               ppc=ppc, ntok=ntok, lam_init=lam_init)
            o = od.reshape(nseq, kvh, GROUP, ntok, VAL_DIM).transpose(3, 0, 1, 2, 4).reshape(ns, e)
            ys = _attn_out(o.astype(BF16), gs, wout, ys, fg, tm=ns)

    y_prompt = yp.reshape(batch, seq, d)
    y_sample = jnp.swapaxes(ys.reshape(ntok, nseq, d), 0, 1)
    rows_p = lambda a: a.reshape(n_attn, batch, seq, kvh, VAL_DIM)
    rows_s = lambda a: jnp.swapaxes(a.reshape(n_attn, ntok, nseq, kvh, VAL_DIM), 1, 2)
    return (y_prompt, y_sample, jnp.stack(conv_p), jnp.stack(conv_s),
            rows_p(kv_p[0]), rows_p(kv_p[1]), rows_s(kv_s[0]), rows_s(kv_s[1]))
```

```python
import functools
import math

import jax
import jax.numpy as jnp
from jax import lax
from jax.experimental import pallas as pl
from jax.experimental.pallas import tpu as pltpu

F32 = jnp.float32
BF16 = jnp.bfloat16

N_MIXERS = 2
CONV_WIDTH = 31
HEAD_DIM = 64
VAL_DIM = 2 * HEAD_DIM
GROUP = 4
ROT_DIM = HEAD_DIM // 4
ROPE_THETA = 500000.0
RMS_EPS = 1e-6
LN_EPS = 1e-5
SCALE = HEAD_DIM ** -0.5
Q_SCALE = SCALE * math.log2(math.e)
PAGE_SIZE = 128

LANES = 128
HALO = 32
VMEM_LIMIT = 56 * 1024 * 1024


def _cparams(sem):
    return pltpu.CompilerParams(dimension_semantics=sem, vmem_limit_bytes=VMEM_LIMIT)


def _const_spec(shape, single=False):
    nd = len(shape)
    return pl.BlockSpec(shape, lambda *_: (0,) * nd, pipeline_mode=pl.Buffered(1) if single else None)


def _rms(x, g):
    return x * lax.rsqrt(jnp.mean(x * x, axis=-1, keepdims=True) + RMS_EPS) * g


def _sigmoid(x):
    return 1.0 / (1.0 + jnp.exp(-x))


def _conv_tail(cb_ref, m_ref, gate, lng_ref, lnb_ref, n_chunks):
    tot = cb_ref[0]
    for c in range(1, n_chunks):
        tot = tot + cb_ref[c]
    width = n_chunks * LANES
    mu = jnp.sum(tot, axis=-1, keepdims=True) / width
    sq = None
    for c in range(n_chunks):
        d = cb_ref[c] - mu
        sq = d * d if sq is None else sq + d * d
    rstd = lax.rsqrt(jnp.sum(sq, axis=-1, keepdims=True) / width + LN_EPS)
    for c in range(n_chunks):
        sl = slice(c * LANES, (c + 1) * LANES)
        cn = (cb_ref[c] - mu) * rstd * lng_ref[:, sl] + lnb_ref[:, sl]
        act = cn * _sigmoid(cn)
        g = gate[:, sl]
        m_ref[:, sl] = (act * (g * _sigmoid(g))).astype(BF16)


def _conv_prompt_kernel(x_ref, ng_ref, win_ref, dww_ref, dwb_ref, lng_ref, lnb_ref, wout_ref,
                        y_ref, st_ref, upad_ref, cb_ref, m_ref, *, tm, nt, e):
    t = pl.program_id(1)
    n_chunks = e // LANES

    @pl.when(t == 0)
    def _():
        upad_ref[:, 0:HALO, :] = jnp.zeros((n_chunks, HALO, LANES), F32)

    x = x_ref[...]
    h = _rms(x, ng_ref[...]).astype(BF16)
    a = jnp.dot(h, win_ref[:, 0:e], preferred_element_type=F32)
    b = jnp.dot(h, win_ref[:, e:2 * e], preferred_element_type=F32)
    u = a * _sigmoid(b)
    for c in range(n_chunks):
        upad_ref[c, HALO:HALO + tm, :] = u[:, c * LANES:(c + 1) * LANES]

    off = HALO - (CONV_WIDTH - 1)

    def chunk(c, carry):
        acc = jnp.broadcast_to(dwb_ref[c], (tm, LANES))
        for k in range(CONV_WIDTH):
            acc = acc + dww_ref[c, k:k + 1, :] * upad_ref[c, off + k:off + k + tm, :]
        cb_ref[c] = acc
        return carry

    lax.fori_loop(0, n_chunks, chunk, 0)

    gate = jnp.dot(h, win_ref[:, 2 * e:3 * e], preferred_element_type=F32)
    _conv_tail(cb_ref, m_ref, gate, lng_ref, lnb_ref, n_chunks)
    y_ref[...] = jnp.dot(m_ref[...], wout_ref[...], preferred_element_type=F32) + x

    @pl.when(t == nt - 1)
    def _():
        for c in range(n_chunks):
            st_ref[0, :, c * LANES:(c + 1) * LANES] = upad_ref[c, tm + off:tm + HALO, :]

    upad_ref[:, 0:HALO, :] = upad_ref[:, tm:tm + HALO, :]


def _conv_prompt(x, ng, win, dww3, dwb3, lng, lnb, wout, *, batch, seq, tm):
    n, d = x.shape
    e = wout.shape[0]
    nt = seq // tm
    n_chunks = e // LANES
    kern = functools.partial(_conv_prompt_kernel, tm=tm, nt=nt, e=e)
    return pl.pallas_call(
        kern,
        grid=(batch, nt),
        in_specs=[
            pl.BlockSpec((tm, d), lambda b, t: (b * nt + t, 0)),
            _const_spec((1, d)),
            _const_spec(win.shape, single=True),
            _const_spec(dww3.shape),
            _const_spec(dwb3.shape),
            _const_spec((1, e)),
            _const_spec((1, e)),
            _const_spec(wout.shape, single=True),
        ],
        out_specs=[
            pl.BlockSpec((tm, d), lambda b, t: (b * nt + t, 0)),
            pl.BlockSpec((1, CONV_WIDTH - 1, e), lambda b, t: (b, 0, 0)),
        ],
        out_shape=[
            jax.ShapeDtypeStruct((n, d), F32),
            jax.ShapeDtypeStruct((batch, CONV_WIDTH - 1, e), F32),
        ],
        scratch_shapes=[
            pltpu.VMEM((n_chunks, HALO + tm, LANES), F32),
            pltpu.VMEM((n_chunks, tm, LANES), F32),
            pltpu.VMEM((tm, e), BF16),
        ],
        compiler_params=_cparams(("arbitrary", "arbitrary")),
        name="conv_prompt",
    )(x, ng, win, dww3, dwb3, lng, lnb, wout)


def _conv_sample_kernel(x_ref, ng_ref, win_ref, prev_ref, dww_ref, dwb_ref, lng_ref, lnb_ref, wout_ref,
                        y_ref, st_ref, u_ref, cb_ref, m_ref, *, nseq, ntok, e):
    n_chunks = e // LANES
    hist = CONV_WIDTH - 1
    x = x_ref[...]
    h = _rms(x, ng_ref[...]).astype(BF16)
    a = jnp.dot(h, win_ref[:, 0:e], preferred_element_type=F32)
    b = jnp.dot(h, win_ref[:, e:2 * e], preferred_element_type=F32)
    u_ref[...] = a * _sigmoid(b)

    def row(r, sl):
        if r < hist:
            return prev_ref[r, :, sl]
        return u_ref[(r - hist) * nseq:(r - hist + 1) * nseq, sl]

    for c in range(n_chunks):
        sl = slice(c * LANES, (c + 1) * LANES)
        for t in range(ntok):
            acc = jnp.broadcast_to(dwb_ref[c], (nseq, LANES))
            for k in range(CONV_WIDTH):
                acc = acc + dww_ref[c, k:k + 1, :] * row(t + k, sl)
            cb_ref[c, t * nseq:(t + 1) * nseq, :] = acc
        for r in range(hist):
            st_ref[r, :, sl] = row(r + ntok, sl)

    gate = jnp.dot(h, win_ref[:, 2 * e:3 * e], preferred_element_type=F32)
    _conv_tail(cb_ref, m_ref, gate, lng_ref, lnb_ref, n_chunks)
    y_ref[...] = jnp.dot(m_ref[...], wout_ref[...], preferred_element_type=F32) + x


def _conv_sample(x, ng, win, prev_t, dww3, dwb3, lng, lnb, wout, *, nseq, ntok):
    n, d = x.shape
    e = wout.shape[0]
    n_chunks = e // LANES
    kern = functools.partial(_conv_sample_kernel, nseq=nseq, ntok=ntok, e=e)
    args = (x, ng, win, prev_t, dww3, dwb3, lng, lnb, wout)
    return pl.pallas_call(
        kern,
        grid=(1,),
        in_specs=[_const_spec(a.shape) for a in args],
        out_specs=[_const_spec((n, d)), _const_spec(prev_t.shape)],
        out_shape=[jax.ShapeDtypeStruct((n, d), F32), jax.ShapeDtypeStruct(prev_t.shape, F32)],
        scratch_shapes=[
            pltpu.VMEM((n, e), F32),
            pltpu.VMEM((n_chunks, n, LANES), F32),
            pltpu.VMEM((n, e), BF16),
        ],
        compiler_params=_cparams(("arbitrary",)),
        name="conv_sample",
    )(*args)


def _rope_tile(x, cos, sin_up, sin_dn):
    return (x * cos + pltpu.roll(x, LANES - ROT_DIM // 2, axis=1) * sin_up
            + pltpu.roll(x, ROT_DIM // 2, axis=1) * sin_dn)


def _attn_proj_kernel(x_ref, ng_ref, win_ref, cos_ref, sup_ref, sdn_ref, *rest, qw, kvw, transpose_v, aliased):
    q_ref, kf_ref, kb_ref, vf_ref, vb_ref, gs_ref = rest[2:] if aliased else rest
    kvh = kvw // LANES
    tm = x_ref.shape[0]
    x = x_ref[...]
    h = _rms(x, ng_ref[...]).astype(BF16)
    cos, sup, sdn = cos_ref[...], sup_ref[...], sdn_ref[...]
    q = jnp.dot(h, win_ref[:, 0:qw], preferred_element_type=F32)
    for j in range(qw // LANES):
        sl = slice(j * LANES, (j + 1) * LANES)
        q_ref[:, sl] = (_rope_tile(q[:, sl], cos, sup, sdn) * Q_SCALE).astype(BF16)
    k = jnp.dot(h, win_ref[:, qw:qw + kvw], preferred_element_type=F32)
    for j in range(kvh):
        sl = slice(j * LANES, (j + 1) * LANES)
        kr = _rope_tile(k[:, sl], cos, sup, sdn)
        kf_ref[0, pl.ds(j, tm, stride=kvh), :] = kr
        kb_ref[:, sl] = kr.astype(BF16)
    v = jnp.dot(h, win_ref[:, qw + kvw:qw + 2 * kvw], preferred_element_type=F32)
    for j in range(kvh):
        vf_ref[0, pl.ds(j, tm, stride=kvh), :] = v[:, j * LANES:(j + 1) * LANES]
    if transpose_v:
        vt = v.T.astype(BF16)
        for j in range(kvh):
            vb_ref[0, 0, j] = vt[j * LANES:(j + 1) * LANES, :]
    else:
        vb_ref[...] = v.astype(BF16)
    g = jnp.dot(h, win_ref[:, qw + 2 * kvw:], preferred_element_type=F32)
    gs_ref[...] = (g * _sigmoid(g)).astype(BF16)


def _attn_proj(x, ng, win, cos, sup, sdn, kv_rows, *, layer, n_layers, qw, kvw, tm, seq_tiles, transpose_v):
    n, d = x.shape
    nt = n // tm
    kvh = kvw // LANES
    gw = win.shape[1] - qw - 2 * kvw
    row = lambda w: pl.BlockSpec((tm, w), lambda i: (i, 0))
    tab = pl.BlockSpec((tm, LANES), lambda i: (i % seq_tiles, 0))
    if transpose_v:
        vb_spec = pl.BlockSpec((1, 1, kvh, LANES, tm), lambda i: (i // seq_tiles, i % seq_tiles, 0, 0, 0))
        vb_shape = jax.ShapeDtypeStruct((nt // seq_tiles, seq_tiles, kvh, LANES, tm), BF16)
    else:
        vb_spec = row(kvw)
        vb_shape = jax.ShapeDtypeStruct((n, kvw), BF16)
    aliased = kv_rows is not None
    kern = functools.partial(_attn_proj_kernel, qw=qw, kvw=kvw, transpose_v=transpose_v, aliased=aliased)
    rows_spec = pl.BlockSpec((1, tm * kvh, LANES), lambda i: (layer, i, 0))
    rows_shape = jax.ShapeDtypeStruct((n_layers, n * kvh, LANES), F32)
    in_specs = [row(d), _const_spec((1, d)), _const_spec(win.shape), tab, tab, tab]
    args = [x, ng, win, cos, sup, sdn]
    if aliased:
        in_specs += [pl.BlockSpec(memory_space=pl.ANY)] * 2
        args += list(kv_rows)
    return pl.pallas_call(
        kern,
        grid=(nt,),
        in_specs=in_specs,
        out_specs=[row(qw), rows_spec, row(kvw), rows_spec, vb_spec, row(gw)],
        out_shape=[
            jax.ShapeDtypeStruct((n, qw), BF16),
            rows_shape,
            jax.ShapeDtypeStruct((n, kvw), BF16),
            rows_shape,
            vb_shape,
            jax.ShapeDtypeStruct((n, gw), BF16),
        ],
        input_output_aliases={6: 1, 7: 3} if aliased else {},
        compiler_params=_cparams(("arbitrary",)),
        name="attn_proj",
    )(*args)


def _diff_lambda(lq1_ref, lk1_ref, lq2_ref, lk2_ref, lam_init):
    s1 = jnp.sum(lq1_ref[...] * lk1_ref[...], axis=-1, keepdims=True)
    s2 = jnp.sum(lq2_ref[...] * lk2_ref[...], axis=-1, keepdims=True)
    return jnp.exp(s1) - jnp.exp(s2) + lam_init


def _flash_prompt_kernel(q_ref, qn_ref, k_ref, vt_ref, lq1_ref, lk1_ref, lq2_ref, lk2_ref, sg_ref, o_ref,
                         qp_ref, m_ref, l_ref, acc_ref, s0_ref, s1_ref, s2_ref, x0_ref, x1_ref, x2_ref,
                         *, tq, sb, ct, lam_init):
    qi = pl.program_id(2)
    r = 2 * GROUP * tq
    n_ct = r // ct
    nt_dims = (((1,), (1,)), ((), ()))

    def pad_queries(src_ref):
        lane = lax.broadcasted_iota(jnp.int32, (tq, LANES), 1)
        zero = jnp.zeros((tq, LANES), BF16)
        for g in range(GROUP):
            qg = src_ref[:, g * LANES:(g + 1) * LANES]
            qp_ref[(2 * g) * tq:(2 * g + 1) * tq, :] = jnp.where(lane < HEAD_DIM, qg, zero)
            qp_ref[(2 * g + 1) * tq:(2 * g + 2) * tq, :] = jnp.where(lane >= HEAD_DIM, qg, zero)

    def step(nxt, cur):
        if nxt is not None:
            sn_ref, xn_ref, jn = nxt
            kb = k_ref[pl.ds(pl.multiple_of(jn * sb, sb), sb), :]
        if cur is not None:
            sc_ref, xc_ref, jc, masked = cur
            vt = vt_ref[0, jc, 0]
        for c in range(n_ct):
            cs = slice(c * ct, (c + 1) * ct)
            if nxt is not None:
                sn = lax.dot_general(kb, qp_ref[cs, :], nt_dims, preferred_element_type=F32)
                sn_ref[:, cs] = sn
                xn_ref[:, cs] = jnp.max(sn, axis=0, keepdims=True)
            if cur is None:
                continue
            s = sc_ref[:, cs]
            if masked:
                key = jc * sb + lax.broadcasted_iota(jnp.int32, (sb, ct), 0)
                tok = qi * tq + lax.rem(c * ct + lax.broadcasted_iota(jnp.int32, (sb, ct), 1), tq)
                s = jnp.where(key <= tok, s, -jnp.inf)
                mx = jnp.max(s, axis=0, keepdims=True)
            else:
                mx = xc_ref[:, cs]
            m_old = m_ref[:, cs]
            m_new = jnp.maximum(m_old, mx)
            alpha = jnp.exp2(m_old - m_new)
            p = jnp.exp2(s - m_new)
            l_ref[:, cs] = alpha * l_ref[:, cs] + jnp.sum(p, axis=0, keepdims=True)
            pv = jnp.dot(vt, p.astype(BF16), preferred_element_type=F32)
            acc_ref[:, cs] = alpha * acc_ref[:, cs] + pv
            m_ref[:, cs] = m_new

    b0 = (s0_ref, x0_ref)
    b1 = (s1_ref, x1_ref)
    b2 = (s2_ref, x2_ref)

    @pl.when(qi == 0)
    def _():
        pad_queries(q_ref)
        step(b2 + (0,), None)

    m_ref[...] = jnp.full((1, r), -jnp.inf, F32)
    l_ref[...] = jnp.zeros((1, r), F32)
    acc_ref[...] = jnp.zeros((VAL_DIM, r), F32)

    nv = (qi * tq) // sb

    @pl.when(nv >= 1)
    def _():
        step(b1 + (1,), b2 + (0, False))

    rest = jnp.maximum(nv - 1, 0)

    def pair(jj, carry):
        j0 = 1 + 2 * jj
        step(b0 + (j0 + 1,), b1 + (j0, False))
        step(b1 + (j0 + 2,), b0 + (j0 + 1, False))
        return carry

    lax.fori_loop(0, rest // 2, pair, 0)
    rest_odd = lax.rem(rest, 2) == 1

    @pl.when(nv == 0)
    def _():
        step(None, b2 + (0, True))
        pad_queries(qn_ref)
        step(b2 + (0,), None)

    @pl.when(jnp.logical_and(nv >= 1, jnp.logical_not(rest_odd)))
    def _():
        pad_queries(qn_ref)
        step(b2 + (0,), b1 + (nv, True))

    @pl.when(jnp.logical_and(nv >= 1, rest_odd))
    def _():
        step(b0 + (nv,), b1 + (nv - 1, False))
        pad_queries(qn_ref)
        step(b2 + (0,), b0 + (nv, True))

    lam = _diff_lambda(lq1_ref, lk1_ref, lq2_ref, lk2_ref, lam_init)
    for g in range(GROUP):
        s1 = slice((2 * g) * tq, (2 * g + 1) * tq)
        s2 = slice((2 * g + 1) * tq, (2 * g + 2) * tq)
        o1 = acc_ref[:, s1] * (1.0 / l_ref[:, s1])
        o2 = acc_ref[:, s2] * (1.0 / l_ref[:, s2])
        d = o1 - lam * o2
        dn = d * lax.rsqrt(jnp.mean(d * d, axis=0, keepdims=True) + RMS_EPS)
        o_ref[:, g * LANES:(g + 1) * LANES] = ((dn.T * sg_ref[...]) * (1.0 - lam_init)).astype(BF16)


def _flash_prompt(q, kb, vt5, lams, sg, *, batch, seq, tq, ct, lam_init):
    n, qw = q.shape
    kvh = kb.shape[1] // LANES
    sb = vt5.shape[-1]
    nq = seq // tq
    r = 2 * GROUP * tq
    kern = functools.partial(_flash_prompt_kernel, tq=tq, sb=sb, ct=ct, lam_init=lam_init)
    vec = _const_spec((1, HEAD_DIM))
    return pl.pallas_call(
        kern,
        grid=(batch, kvh, nq),
        in_specs=[
            pl.BlockSpec((tq, GROUP * LANES), lambda b, h, i: (b * nq + i, h)),
            pl.BlockSpec((tq, GROUP * LANES), lambda b, h, i: (b * nq + jnp.minimum(i + 1, nq - 1), h)),
            pl.BlockSpec((seq, LANES), lambda b, h, i: (b, h)),
            pl.BlockSpec((1, seq // sb, 1, LANES, sb), lambda b, h, i: (b, 0, h, 0, 0)),
            vec, vec, vec, vec,
            _const_spec((1, VAL_DIM)),
        ],
        out_specs=pl.BlockSpec((tq, GROUP * LANES), lambda b, h, i: (b * nq + i, h)),
        out_shape=jax.ShapeDtypeStruct((n, qw), BF16),
        scratch_shapes=[
            pltpu.VMEM((r, LANES), BF16),
            pltpu.VMEM((1, r), F32),
            pltpu.VMEM((1, r), F32),
            pltpu.VMEM((VAL_DIM, r), F32),
            pltpu.VMEM((sb, r), F32),
            pltpu.VMEM((sb, r), F32),
            pltpu.VMEM((sb, r), F32),
            pltpu.VMEM((1, r), F32),
            pltpu.VMEM((1, r), F32),
            pltpu.VMEM((1, r), F32),
        ],
        compiler_params=_cparams(("arbitrary", "arbitrary", "arbitrary")),
        name="flash_prompt",
    )(q, q, kb, vt5, *lams, sg)


def _flash_sample_kernel(pt_ref, q_ref, *refs, ppc, ntok, lam_init):
    del pt_ref
    k_refs = refs[0:ppc]
    v_refs = refs[ppc:2 * ppc]
    kn_ref, vn_ref, lq1_ref, lk1_ref, lq2_ref, lk2_ref, sg_ref, o_ref, m_ref, l_ref, acc_ref = refs[2 * ppc:]
    c = pl.program_id(1)
    rows = q_ref.shape[1]
    kvh = q_ref.shape[2] // LANES
    rpk = rows // kvh

    @pl.when(c == 0)
    def _():
        m_ref[...] = jnp.full(m_ref.shape, -jnp.inf, F32)
        l_ref[...] = jnp.zeros(l_ref.shape, F32)
        acc_ref[...] = jnp.zeros(acc_ref.shape, F32)

    q = q_ref[0]

    def update(s_list, v_list):
        s = jnp.concatenate(s_list, axis=1) if len(s_list) > 1 else s_list[0]
        m_old = m_ref[...]
        m_new = jnp.maximum(m_old, jnp.max(s, axis=1, keepdims=True))
        alpha = jnp.exp2(m_old - m_new)
        p = jnp.exp2(s - m_new)
        l_ref[...] = alpha * l_ref[...] + jnp.sum(p, axis=1, keepdims=True)
        pb = p.astype(BF16)
        pv = None
        for i, v in enumerate(v_list):
            d = jnp.dot(pb[:, i * PAGE_SIZE:(i + 1) * PAGE_SIZE], v, preferred_element_type=F32)
            pv = d if pv is None else pv + d
        acc_ref[...] = alpha * acc_ref[...] + pv
        m_ref[...] = m_new

    def page(ref):
        heads = [ref[0, 0, pl.ds(h, PAGE_SIZE, stride=kvh), :] for h in range(kvh)]
        return jnp.concatenate(heads, axis=1).astype(BF16)

    nt_dims = (((1,), (1,)), ((), ()))
    s_list = [lax.dot_general(q, page(k_refs[i]), nt_dims, preferred_element_type=F32) for i in range(ppc)]
    update(s_list, [page(v_refs[i]) for i in range(ppc)])

    @pl.when(c == pl.num_programs(1) - 1)
    def _():
        s = lax.dot_general(q, kn_ref[0], nt_dims, preferred_element_type=F32)
        key = lax.broadcasted_iota(jnp.int32, s.shape, 1)
        tok = lax.rem(lax.broadcasted_iota(jnp.int32, s.shape, 0), ntok)
        update([jnp.where(key <= tok, s, -jnp.inf)], [vn_ref[0]])
        lam = _diff_lambda(lq1_ref, lk1_ref, lq2_ref, lk2_ref, lam_init)
        half = rpk // 2
        for h in range(kvh):
            blk = (acc_ref[h * rpk:(h + 1) * rpk, h * LANES:(h + 1) * LANES]
                   / l_ref[h * rpk:(h + 1) * rpk, :])
            d = blk[0:half] - lam * blk[half:rpk]
            dn = d * lax.rsqrt(jnp.mean(d * d, axis=-1, keepdims=True) + RMS_EPS)
            o_ref[0, h * half:(h + 1) * half, :] = (dn * sg_ref[...]) * (1.0 - lam_init)


def _flash_sample(page_table, qblk, cache_k4, cache_v4, layer, knew, vnew, lams, sg, *, ppc, ntok, lam_init):
    nseq, rows, kw = qblk.shape
    n_pages = page_table.shape[1]
    nch = n_pages // ppc
    kvh = kw // LANES
    kern = functools.partial(_flash_sample_kernel, ppc=ppc, ntok=ntok, lam_init=lam_init)

    def page_spec(i):
        return pl.BlockSpec((1, 1, PAGE_SIZE * kvh, LANES), lambda n, c, pt: (layer, pt[n, c * ppc + i], 0, 0))

    seq_spec = pl.BlockSpec((1, PAGE_SIZE, kw), lambda n, c, pt: (n, 0, 0))
    vec = pl.BlockSpec((1, HEAD_DIM), lambda n, c, pt: (0, 0))
    out_rows = rows // 2
    grid_spec = pltpu.PrefetchScalarGridSpec(
        num_scalar_prefetch=1,
        grid=(nseq, nch),
        in_specs=([pl.BlockSpec((1, rows, kw), lambda n, c, pt: (n, 0, 0))]
                  + [page_spec(i) for i in range(ppc)] + [page_spec(i) for i in range(ppc)]
                  + [seq_spec, seq_spec, vec, vec, vec, vec,
                     pl.BlockSpec((1, VAL_DIM), lambda n, c, pt: (0, 0))]),
        out_specs=pl.BlockSpec((1, out_rows, VAL_DIM), lambda n, c, pt: (n, 0, 0)),
        scratch_shapes=[
            pltpu.VMEM((rows, 1), F32),
            pltpu.VMEM((rows, 1), F32),
            pltpu.VMEM((rows, kw), F32),
        ],
    )
    return pl.pallas_call(
        kern,
        grid_spec=grid_spec,
        out_shape=jax.ShapeDtypeStruct((nseq, out_rows, VAL_DIM), F32),
        compiler_params=_cparams(("arbitrary", "arbitrary")),
        name="flash_sample",
    )(page_table, qblk, *([cache_k4] * ppc), *([cache_v4] * ppc), knew, vnew, *lams, sg)


def _attn_out_kernel(o_ref, gs_ref, w_ref, x_ref, *rest, final):
    if final:
        fg_ref, y_ref = rest
    else:
        (y_ref,) = rest
    y = jnp.dot(o_ref[...] * gs_ref[...], w_ref[...], preferred_element_type=F32) + x_ref[...]
    if final:
        y = _rms(y, fg_ref[...])
    y_ref[...] = y


def _attn_out(o, gs, w, x, fg, *, tm):
    n, d = x.shape
    e = w.shape[0]
    final = fg is not None
    row = lambda wd: pl.BlockSpec((tm, wd), lambda i: (i, 0))
    in_specs = [row(e), row(e), _const_spec(w.shape), row(d)]
    args = [o, gs, w, x]
    if final:
        in_specs.append(_const_spec((1, d)))
        args.append(fg)
    return pl.pallas_call(
        functools.partial(_attn_out_kernel, final=final),
        grid=(n // tm,),
        in_specs=in_specs,
        out_specs=row(d),
        out_shape=jax.ShapeDtypeStruct((n, d), F32),
        compiler_params=_cparams(("arbitrary",)),
        name="attn_out",
    )(*args)


def _final_norm_kernel(x_ref, g_ref, y_ref):
    y_ref[...] = _rms(x_ref[...], g_ref[...])


def _final_norm(x, g, *, tm):
    n, d = x.shape
    row = pl.BlockSpec((tm, d), lambda i: (i, 0))
    return pl.pallas_call(
        _final_norm_kernel, grid=(n // tm,), in_specs=[row, _const_spec((1, d))], out_specs=row,
        out_shape=jax.ShapeDtypeStruct((n, d), F32), compiler_params=_cparams(("arbitrary",)),
        name="final_norm",
    )(x, g)


def _rope_tables(pos):
    half = ROT_DIM // 2
    inv_freq = ROPE_THETA ** (-jnp.arange(half, dtype=F32) / half)
    ang = pos.astype(F32)[:, None] * inv_freq[None, :]
    cos, sin = jnp.cos(ang), jnp.sin(ang)
    n = pos.shape[0]
    pad = jnp.zeros((n, HEAD_DIM - ROT_DIM), F32)
    zero = jnp.zeros((n, half), F32)
    c64 = jnp.concatenate([cos, cos, pad + 1.0], axis=1)
    up64 = jnp.concatenate([-sin, zero, pad], axis=1)
    dn64 = jnp.concatenate([zero, sin, pad], axis=1)
    rep = lambda t: jnp.concatenate([t, t], axis=1)
    return rep(c64), rep(up64), rep(dn64)


def _pick_tile(n, pref):
    t = min(n, pref)
    while n % t:
        t //= 2
    return t


def kernel(x_prompt, x_sample, state_conv, cache_k, cache_v, page_table, norm_g, final_norm_g,
           conv_w_in, conv_dw_w, conv_dw_b, conv_ln_g, conv_ln_b, conv_w_out,
           attn_w_in, attn_lambda_q1, attn_lambda_k1, attn_lambda_q2, attn_lambda_k2,
           attn_subln_g, attn_w_out):
    batch, seq, d = x_prompt.shape
    nseq, ntok, _ = x_sample.shape
    depth = norm_g.shape[0]
    e = conv_w_out.shape[1]
    n_heads = e // VAL_DIM
    kvh = n_heads // GROUP
    qw = n_heads * 2 * HEAD_DIM
    kvw = kvh * 2 * HEAD_DIM
    n_chunks = e // LANES
    n_pages = page_table.shape[1]
    past = n_pages * PAGE_SIZE

    tm_conv = _pick_tile(seq, 512)
    tm_proj = _pick_tile(seq, 512)
    tq = _pick_tile(seq, 512)
    ct = 256
    ppc = _pick_tile(n_pages, 16)

    yp = x_prompt.reshape(batch * seq, d)
    ys = jnp.swapaxes(x_sample, 0, 1).reshape(ntok * nseq, d)
    ns = ntok * nseq

    tabs_p = _rope_tables(jnp.arange(seq, dtype=jnp.int32))
    tabs_s = _rope_tables(past + jnp.repeat(jnp.arange(ntok, dtype=jnp.int32), nseq))

    cache_k4 = cache_k.reshape(cache_k.shape[0], cache_k.shape[1], PAGE_SIZE * kvh, 2 * HEAD_DIM)
    cache_v4 = cache_v.reshape(cache_v.shape[0], cache_v.shape[1], PAGE_SIZE * kvh, VAL_DIM)
    eye_h = jnp.eye(kvh, dtype=BF16)
    eye_c = jnp.eye(2, dtype=BF16)

    n_attn = depth // N_MIXERS
    conv_p, conv_s = [], []
    kv_p = kv_s = None
    for i in range(depth):
        j = i // N_MIXERS
        ng = norm_g[i].reshape(1, d)
        last = i == depth - 1
        if i % N_MIXERS == 0:
            win = conv_w_in[j].astype(BF16)
            wout = conv_w_out[j].astype(BF16)
            dww3 = jnp.pad(conv_dw_w[j], ((0, HALO - CONV_WIDTH), (0, 0))).reshape(HALO, n_chunks, LANES)
            dww3 = jnp.swapaxes(dww3, 0, 1)
            dwb3 = conv_dw_b[j].reshape(n_chunks, 1, LANES)
            lng = conv_ln_g[j].reshape(1, e)
            lnb = conv_ln_b[j].reshape(1, e)
            yp, sp = _conv_prompt(yp, ng, win, dww3, dwb3, lng, lnb, wout, batch=batch, seq=seq, tm=tm_conv)
            prev_t = jnp.swapaxes(state_conv[j], 0, 1)
            ys, ss_t = _conv_sample(ys, ng, win, prev_t, dww3, dwb3, lng, lnb, wout, nseq=nseq, ntok=ntok)
            conv_p.append(sp)
            conv_s.append(jnp.swapaxes(ss_t, 0, 1))
            if last:
                yp = _final_norm(yp, final_norm_g.reshape(1, d), tm=tm_proj)
                ys = _final_norm(ys, final_norm_g.reshape(1, d), tm=ns)
        else:
            lam_init = 0.8 - 0.6 * math.exp(-0.3 * i)
            win = attn_w_in[j].astype(BF16)
            wout = attn_w_out[j].astype(BF16)
            lams = [a[j].reshape(1, HEAD_DIM) for a in
                    (attn_lambda_q1, attn_lambda_k1, attn_lambda_q2, attn_lambda_k2)]
            sg = attn_subln_g[j].reshape(1, VAL_DIM)
            fg = final_norm_g.reshape(1, d) if last else None

            q, kf, kb, vf, vt5, gs = _attn_proj(yp, ng, win, *tabs_p, kv_p, layer=j, n_layers=n_attn, qw=qw,
                                                kvw=kvw, tm=tm_proj, seq_tiles=seq // tm_proj, transpose_v=True)
            kv_p = (kf, vf)
            o = _flash_prompt(q, kb, vt5, lams, sg, batch=batch, seq=seq, tq=tq, ct=ct, lam_init=lam_init)
            yp = _attn_out(o, gs, wout, yp, fg, tm=tm_proj)

            q, kf, kb, vf, vb, gs = _attn_proj(ys, ng, win, *tabs_s, kv_s, layer=j, n_layers=n_attn, qw=qw,
                                               kvw=kvw, tm=ns, seq_tiles=1, transpose_v=False)
            kv_s = (kf, vf)
            q6 = q.reshape(ntok, nseq, kvh, GROUP, 2, HEAD_DIM).transpose(1, 2, 4, 3, 0, 5)
            q6 = q6.reshape(nseq, kvh, 2, GROUP * ntok, HEAD_DIM)
            qblk = (q6[:, :, :, :, None, None, :] * eye_h[None, :, None, None, :, None, None]
                    * eye_c[None, None, :, None, None, :, None])
            qblk = qblk.reshape(nseq, kvh * 2 * GROUP * ntok, kvw)
            pad_new = lambda a: jnp.pad(jnp.swapaxes(a.reshape(ntok, nseq, kvw), 0, 1),
                                        ((0, 0), (0, PAGE_SIZE - ntok), (0, 0)))
            od = _flash_sample(page_table, qblk, cache_k4, cache_v4, j, pad_new(kb), pad_new(vb), lams, sg,
                               ppc=ppc, ntok=ntok, lam_init=lam_init)
            o = od.reshape(nseq, kvh, GROUP, ntok, VAL_DIM).transpose(3, 0, 1, 2, 4).reshape(ns, e)
            ys = _attn_out(o.astype(BF16), gs, wout, ys, fg, tm=ns)

    y_prompt = yp.reshape(batch, seq, d)
    y_sample = jnp.swapaxes(ys.reshape(ntok, nseq, d), 0, 1)
    rows_p = lambda a: a.reshape(n_attn, batch, seq, kvh, VAL_DIM)
    rows_s = lambda a: jnp.swapaxes(a.reshape(n_attn, ntok, nseq, kvh, VAL_DIM), 1, 2)
    return (y_prompt, y_sample, jnp.stack(conv_p), jnp.stack(conv_s),
            rows_p(kv_p[0]), rows_p(kv_p[1]), rows_s(kv_s[0]), rows_s(kv_s[1]))
```

```python
import functools
import math

import jax
import jax.numpy as jnp
from jax import lax
from jax.experimental import pallas as pl
from jax.experimental.pallas import tpu as pltpu

F32 = jnp.float32
BF16 = jnp.bfloat16

N_MIXERS = 2
CONV_WIDTH = 31
HEAD_DIM = 64
VAL_DIM = 2 * HEAD_DIM
GROUP = 4
ROT_DIM = HEAD_DIM // 4
ROPE_THETA = 500000.0
RMS_EPS = 1e-6
LN_EPS = 1e-5
SCALE = HEAD_DIM ** -0.5
Q_SCALE = SCALE * math.log2(math.e)
PAGE_SIZE = 128

LANES = 128
HALO = 32
VMEM_LIMIT = 56 * 1024 * 1024


def _cparams(sem):
    return pltpu.CompilerParams(dimension_semantics=sem, vmem_limit_bytes=VMEM_LIMIT)


def _const_spec(shape, single=False):
    nd = len(shape)
    return pl.BlockSpec(shape, lambda *_: (0,) * nd, pipeline_mode=pl.Buffered(1) if single else None)


def _rms(x, g):
    return x * lax.rsqrt(jnp.mean(x * x, axis=-1, keepdims=True) + RMS_EPS) * g


def _sigmoid(x):
    return 1.0 / (1.0 + jnp.exp(-x))


def _conv_tail(cb_ref, m_ref, gate, lng_ref, lnb_ref, n_chunks):
    tot = cb_ref[0]
    for c in range(1, n_chunks):
        tot = tot + cb_ref[c]
    width = n_chunks * LANES
    mu = jnp.sum(tot, axis=-1, keepdims=True) / width
    sq = None
    for c in range(n_chunks):
        d = cb_ref[c] - mu
        sq = d * d if sq is None else sq + d * d
    rstd = lax.rsqrt(jnp.sum(sq, axis=-1, keepdims=True) / width + LN_EPS)
    for c in range(n_chunks):
        sl = slice(c * LANES, (c + 1) * LANES)
        cn = (cb_ref[c] - mu) * rstd * lng_ref[:, sl] + lnb_ref[:, sl]
        act = cn * _sigmoid(cn)
        g = gate[:, sl]
        m_ref[:, sl] = (act * (g * _sigmoid(g))).astype(BF16)


def _conv_prompt_kernel(x_ref, ng_ref, win_ref, dww_ref, dwb_ref, lng_ref, lnb_ref, wout_ref,
                        y_ref, st_ref, upad_ref, cb_ref, m_ref, *, tm, nt, e):
    t = pl.program_id(1)
    n_chunks = e // LANES

    @pl.when(t == 0)
    def _():
        upad_ref[:, 0:HALO, :] = jnp.zeros((n_chunks, HALO, LANES), F32)

    x = x_ref[...]
    h = _rms(x, ng_ref[...]).astype(BF16)
    a = jnp.dot(h, win_ref[:, 0:e], preferred_element_type=F32)
    b = jnp.dot(h, win_ref[:, e:2 * e], preferred_element_type=F32)
    u = a * _sigmoid(b)
    for c in range(n_chunks):
        upad_ref[c, HALO:HALO + tm, :] = u[:, c * LANES:(c + 1) * LANES]

    off = HALO - (CONV_WIDTH - 1)

    def chunk(c, carry):
        acc = jnp.broadcast_to(dwb_ref[c], (tm, LANES))
        for k in range(CONV_WIDTH):
            acc = acc + dww_ref[c, k:k + 1, :] * upad_ref[c, off + k:off + k + tm, :]
        cb_ref[c] = acc
        return carry

    lax.fori_loop(0, n_chunks, chunk, 0)

    gate = jnp.dot(h, win_ref[:, 2 * e:3 * e], preferred_element_type=F32)
    _conv_tail(cb_ref, m_ref, gate, lng_ref, lnb_ref, n_chunks)
    y_ref[...] = jnp.dot(m_ref[...], wout_ref[...], preferred_element_type=F32) + x

    @pl.when(t == nt - 1)
    def _():
        for c in range(n_chunks):
            st_ref[0, :, c * LANES:(c + 1) * LANES] = upad_ref[c, tm + off:tm + HALO, :]

    upad_ref[:, 0:HALO, :] = upad_ref[:, tm:tm + HALO, :]


def _conv_prompt(x, ng, win, dww3, dwb3, lng, lnb, wout, *, batch, seq, tm):
    n, d = x.shape
    e = wout.shape[0]
    nt = seq // tm
    n_chunks = e // LANES
    kern = functools.partial(_conv_prompt_kernel, tm=tm, nt=nt, e=e)
    return pl.pallas_call(
        kern,
        grid=(batch, nt),
        in_specs=[
            pl.BlockSpec((tm, d), lambda b, t: (b * nt + t, 0)),
            _const_spec((1, d)),
            _const_spec(win.shape, single=True),
            _const_spec(dww3.shape),
            _const_spec(dwb3.shape),
            _const_spec((1, e)),
            _const_spec((1, e)),
            _const_spec(wout.shape, single=True),
        ],
        out_specs=[
            pl.BlockSpec((tm, d), lambda b, t: (b * nt + t, 0)),
            pl.BlockSpec((1, CONV_WIDTH - 1, e), lambda b, t: (b, 0, 0)),
        ],
        out_shape=[
            jax.ShapeDtypeStruct((n, d), F32),
            jax.ShapeDtypeStruct((batch, CONV_WIDTH - 1, e), F32),
        ],
        scratch_shapes=[
            pltpu.VMEM((n_chunks, HALO + tm, LANES), F32),
            pltpu.VMEM((n_chunks, tm, LANES), F32),
            pltpu.VMEM((tm, e), BF16),
        ],
        compiler_params=_cparams(("arbitrary", "arbitrary")),
        name="conv_prompt",
    )(x, ng, win, dww3, dwb3, lng, lnb, wout)


def _conv_sample_kernel(x_ref, ng_ref, win_ref, prev_ref, dww_ref, dwb_ref, lng_ref, lnb_ref, wout_ref,
                        y_ref, st_ref, u_ref, cb_ref, m_ref, *, nseq, ntok, e):
    n_chunks = e // LANES
    hist = CONV_WIDTH - 1
    x = x_ref[...]
    h = _rms(x, ng_ref[...]).astype(BF16)
    a = jnp.dot(h, win_ref[:, 0:e], preferred_element_type=F32)
    b = jnp.dot(h, win_ref[:, e:2 * e], preferred_element_type=F32)
    u_ref[...] = a * _sigmoid(b)

    def row(r, sl):
        if r < hist:
            return prev_ref[r, :, sl]
        return u_ref[(r - hist) * nseq:(r - hist + 1) * nseq, sl]

    for c in range(n_chunks):
        sl = slice(c * LANES, (c + 1) * LANES)
        for t in range(ntok):
            acc = jnp.broadcast_to(dwb_ref[c], (nseq, LANES))
            for k in range(CONV_WIDTH):
                acc = acc + dww_ref[c, k:k + 1, :] * row(t + k, sl)
            cb_ref[c, t * nseq:(t + 1) * nseq, :] = acc
        for r in range(hist):
            st_ref[r, :, sl] = row(r + ntok, sl)

    gate = jnp.dot(h, win_ref[:, 2 * e:3 * e], preferred_element_type=F32)
    _conv_tail(cb_ref, m_ref, gate, lng_ref, lnb_ref, n_chunks)
    y_ref[...] = jnp.dot(m_ref[...], wout_ref[...], preferred_element_type=F32) + x


def _conv_sample(x, ng, win, prev_t, dww3, dwb3, lng, lnb, wout, *, nseq, ntok):
    n, d = x.shape
    e = wout.shape[0]
    n_chunks = e // LANES
    kern = functools.partial(_conv_sample_kernel, nseq=nseq, ntok=ntok, e=e)
    args = (x, ng, win, prev_t, dww3, dwb3, lng, lnb, wout)
    return pl.pallas_call(
        kern,
        grid=(1,),
        in_specs=[_const_spec(a.shape) for a in args],
        out_specs=[_const_spec((n, d)), _const_spec(prev_t.shape)],
        out_shape=[jax.ShapeDtypeStruct((n, d), F32), jax.ShapeDtypeStruct(prev_t.shape, F32)],
        scratch_shapes=[
            pltpu.VMEM((n, e), F32),
            pltpu.VMEM((n_chunks, n, LANES), F32),
            pltpu.VMEM((n, e), BF16),
        ],
        compiler_params=_cparams(("arbitrary",)),
        name="conv_sample",
    )(*args)


def _rope_tile(x, cos, sin_up, sin_dn):
    return (x * cos + pltpu.roll(x, LANES - ROT_DIM // 2, axis=1) * sin_up
            + pltpu.roll(x, ROT_DIM // 2, axis=1) * sin_dn)


def _attn_proj_kernel(x_ref, ng_ref, win_ref, cos_ref, sup_ref, sdn_ref, *rest, qw, kvw, transpose_v, aliased):
    q_ref, kf_ref, kb_ref, vf_ref, vb_ref, gs_ref = rest[2:] if aliased else rest
    kvh = kvw // LANES
    tm = x_ref.shape[0]
    x = x_ref[...]
    h = _rms(x, ng_ref[...]).astype(BF16)
    cos, sup, sdn = cos_ref[...], sup_ref[...], sdn_ref[...]
    q = jnp.dot(h, win_ref[:, 0:qw], preferred_element_type=F32)
    for j in range(qw // LANES):
        sl = slice(j * LANES, (j + 1) * LANES)
        q_ref[:, sl] = (_rope_tile(q[:, sl], cos, sup, sdn) * Q_SCALE).astype(BF16)
    k = jnp.dot(h, win_ref[:, qw:qw + kvw], preferred_element_type=F32)
    for j in range(kvh):
        sl = slice(j * LANES, (j + 1) * LANES)
        kr = _rope_tile(k[:, sl], cos, sup, sdn)
        kf_ref[0, pl.ds(j, tm, stride=kvh), :] = kr
        kb_ref[:, sl] = kr.astype(BF16)
    v = jnp.dot(h, win_ref[:, qw + kvw:qw + 2 * kvw], preferred_element_type=F32)
    for j in range(kvh):
        vf_ref[0, pl.ds(j, tm, stride=kvh), :] = v[:, j * LANES:(j + 1) * LANES]
    if transpose_v:
        vt = v.T.astype(BF16)
        for j in range(kvh):
            vb_ref[0, 0, j] = vt[j * LANES:(j + 1) * LANES, :]
    else:
        vb_ref[...] = v.astype(BF16)
    g = jnp.dot(h, win_ref[:, qw + 2 * kvw:], preferred_element_type=F32)
    gs_ref[...] = (g * _sigmoid(g)).astype(BF16)


def _attn_proj(x, ng, win, cos, sup, sdn, kv_rows, *, layer, n_layers, qw, kvw, tm, seq_tiles, transpose_v):
    n, d = x.shape
    nt = n // tm
    kvh = kvw // LANES
    gw = win.shape[1] - qw - 2 * kvw
    row = lambda w: pl.BlockSpec((tm, w), lambda i: (i, 0))
    tab = pl.BlockSpec((tm, LANES), lambda i: (i % seq_tiles, 0))
    if transpose_v:
        vb_spec = pl.BlockSpec((1, 1, kvh, LANES, tm), lambda i: (i // seq_tiles, i % seq_tiles, 0, 0, 0))
        vb_shape = jax.ShapeDtypeStruct((nt // seq_tiles, seq_tiles, kvh, LANES, tm), BF16)
    else:
        vb_spec = row(kvw)
        vb_shape = jax.ShapeDtypeStruct((n, kvw), BF16)
    aliased = kv_rows is not None
    kern = functools.partial(_attn_proj_kernel, qw=qw, kvw=kvw, transpose_v=transpose_v, aliased=aliased)
    rows_spec = pl.BlockSpec((1, tm * kvh, LANES), lambda i: (layer, i, 0))
    rows_shape = jax.ShapeDtypeStruct((n_layers, n * kvh, LANES), F32)
    in_specs = [row(d), _const_spec((1, d)), _const_spec(win.shape), tab, tab, tab]
    args = [x, ng, win, cos, sup, sdn]
    if aliased:
        in_specs += [pl.BlockSpec(memory_space=pl.ANY)] * 2
        args += list(kv_rows)
    return pl.pallas_call(
        kern,
        grid=(nt,),
        in_specs=in_specs,
        out_specs=[row(qw), rows_spec, row(kvw), rows_spec, vb_spec, row(gw)],
        out_shape=[
            jax.ShapeDtypeStruct((n, qw), BF16),
            rows_shape,
            jax.ShapeDtypeStruct((n, kvw), BF16),
            rows_shape,
            vb_shape,
            jax.ShapeDtypeStruct((n, gw), BF16),
        ],
        input_output_aliases={6: 1, 7: 3} if aliased else {},
        compiler_params=_cparams(("arbitrary",)),
        name="attn_proj",
    )(*args)


def _diff_lambda(lq1_ref, lk1_ref, lq2_ref, lk2_ref, lam_init):
    s1 = jnp.sum(lq1_ref[...] * lk1_ref[...], axis=-1, keepdims=True)
    s2 = jnp.sum(lq2_ref[...] * lk2_ref[...], axis=-1, keepdims=True)
    return jnp.exp(s1) - jnp.exp(s2) + lam_init


def _flash_prompt_kernel(q_ref, qn_ref, k_ref, vt_ref, tri_ref, lq1_ref, lk1_ref, lq2_ref, lk2_ref, sg_ref, o_ref,
                         qp_ref, m_ref, l_ref, acc_ref, s0_ref, s1_ref, s2_ref, x0_ref, x1_ref, x2_ref,
                         *, tq, sb, ct, lam_init):
    qi = pl.program_id(2)
    r = 2 * GROUP * tq
    n_ct = r // ct
    nt_dims = (((1,), (1,)), ((), ()))

    def pad_queries(src_ref):
        lane = lax.broadcasted_iota(jnp.int32, (tq, LANES), 1)
        zero = jnp.zeros((tq, LANES), BF16)
        for g in range(GROUP):
            qg = src_ref[:, g * LANES:(g + 1) * LANES]
            qp_ref[(2 * g) * tq:(2 * g + 1) * tq, :] = jnp.where(lane < HEAD_DIM, qg, zero)
            qp_ref[(2 * g + 1) * tq:(2 * g + 2) * tq, :] = jnp.where(lane >= HEAD_DIM, qg, zero)

    def step(nxt, cur, nxt_diag=False):
        if nxt is not None:
            sn_ref, xn_ref, jn = nxt
        if cur is not None:
            sc_ref, xc_ref, jc, diag = cur
        for c in range(n_ct):
            cs = slice(c * ct, (c + 1) * ct)
            t0 = (c * ct) % tq
            if nxt is not None:
                rows = t0 + ct if nxt_diag else sb
                kb = k_ref[pl.ds(pl.multiple_of(jn * sb, sb), rows), :]
                sn = lax.dot_general(kb, qp_ref[cs, :], nt_dims, preferred_element_type=F32)
                sn_ref[0:rows, cs] = sn
                if not nxt_diag:
                    xn_ref[:, cs] = jnp.max(sn, axis=0, keepdims=True)
            if cur is None:
                continue
            if diag:
                rows = t0 + ct
                s = sc_ref[t0:rows, cs] + tri_ref[...]
                if t0:
                    s = jnp.concatenate([sc_ref[0:t0, cs], s], axis=0)
                mx = jnp.max(s, axis=0, keepdims=True)
            else:
                rows = sb
                s = sc_ref[:, cs]
                mx = xc_ref[:, cs]
            vt = vt_ref[0, jc, 0, :, 0:rows]
            m_old = m_ref[:, cs]
            m_new = jnp.maximum(m_old, mx)
            alpha = jnp.exp2(m_old - m_new)
            p = jnp.exp2(s - m_new)
            l_ref[:, cs] = alpha * l_ref[:, cs] + jnp.sum(p, axis=0, keepdims=True)
            pv = jnp.dot(vt, p.astype(BF16), preferred_element_type=F32)
            acc_ref[:, cs] = alpha * acc_ref[:, cs] + pv
            m_ref[:, cs] = m_new

    b0 = (s0_ref, x0_ref)
    b1 = (s1_ref, x1_ref)
    b2 = (s2_ref, x2_ref)

    @pl.when(qi == 0)
    def _():
        pad_queries(q_ref)
        step(b2 + (0,), None)

    m_ref[...] = jnp.full((1, r), -jnp.inf, F32)
    l_ref[...] = jnp.zeros((1, r), F32)
    acc_ref[...] = jnp.zeros((VAL_DIM, r), F32)

    nv = (qi * tq) // sb

    @pl.when(nv >= 1)
    def _():
        step(b1 + (1,), b2 + (0, False))

    rest = jnp.maximum(nv - 1, 0)

    def pair(jj, carry):
        j0 = 1 + 2 * jj
        step(b0 + (j0 + 1,), b1 + (j0, False))
        step(b1 + (j0 + 2,), b0 + (j0 + 1, False))
        return carry

    lax.fori_loop(0, rest // 2, pair, 0)
    rest_odd = lax.rem(rest, 2) == 1

    @pl.when(nv == 0)
    def _():
        step(None, b2 + (0, True))
        pad_queries(qn_ref)
        step(b2 + (0,), None)

    @pl.when(jnp.logical_and(nv >= 1, jnp.logical_not(rest_odd)))
    def _():
        pad_queries(qn_ref)
        step(b2 + (0,), b1 + (nv, True))

    @pl.when(jnp.logical_and(nv >= 1, rest_odd))
    def _():
        step(b0 + (nv,), b1 + (nv - 1, False), nxt_diag=True)
        pad_queries(qn_ref)
        step(b2 + (0,), b0 + (nv, True))

    lam = _diff_lambda(lq1_ref, lk1_ref, lq2_ref, lk2_ref, lam_init)
    for g in range(GROUP):
        s1 = slice((2 * g) * tq, (2 * g + 1) * tq)
        s2 = slice((2 * g + 1) * tq, (2 * g + 2) * tq)
        o1 = acc_ref[:, s1] * (1.0 / l_ref[:, s1])
        o2 = acc_ref[:, s2] * (1.0 / l_ref[:, s2])
        d = o1 - lam * o2
        dn = d * lax.rsqrt(jnp.mean(d * d, axis=0, keepdims=True) + RMS_EPS)
        o_ref[:, g * LANES:(g + 1) * LANES] = ((dn.T * sg_ref[...]) * (1.0 - lam_init)).astype(BF16)


def _flash_prompt(q, kb, vt5, lams, sg, *, batch, seq, tq, ct, lam_init):
    n, qw = q.shape
    kvh = kb.shape[1] // LANES
    sb = vt5.shape[-1]
    nq = seq // tq
    r = 2 * GROUP * tq
    assert tq == sb and tq % ct == 0, (tq, sb, ct)
    kern = functools.partial(_flash_prompt_kernel, tq=tq, sb=sb, ct=ct, lam_init=lam_init)
    vec = _const_spec((1, HEAD_DIM))
    tri = jnp.where(lax.broadcasted_iota(jnp.int32, (ct, ct), 0) <= lax.broadcasted_iota(jnp.int32, (ct, ct), 1),
                    0.0, -jnp.inf).astype(F32)
    return pl.pallas_call(
        kern,
        grid=(batch, kvh, nq),
        in_specs=[
            pl.BlockSpec((tq, GROUP * LANES), lambda b, h, i: (b * nq + i, h)),
            pl.BlockSpec((tq, GROUP * LANES), lambda b, h, i: (b * nq + jnp.minimum(i + 1, nq - 1), h)),
            pl.BlockSpec((seq, LANES), lambda b, h, i: (b, h)),
            pl.BlockSpec((1, seq // sb, 1, LANES, sb), lambda b, h, i: (b, 0, h, 0, 0)),
            _const_spec((ct, ct)),
            vec, vec, vec, vec,
            _const_spec((1, VAL_DIM)),
        ],
        out_specs=pl.BlockSpec((tq, GROUP * LANES), lambda b, h, i: (b * nq + i, h)),
        out_shape=jax.ShapeDtypeStruct((n, qw), BF16),
        scratch_shapes=[
            pltpu.VMEM((r, LANES), BF16),
            pltpu.VMEM((1, r), F32),
            pltpu.VMEM((1, r), F32),
            pltpu.VMEM((VAL_DIM, r), F32),
            pltpu.VMEM((sb, r), F32),
            pltpu.VMEM((sb, r), F32),
            pltpu.VMEM((sb, r), F32),
            pltpu.VMEM((1, r), F32),
            pltpu.VMEM((1, r), F32),
            pltpu.VMEM((1, r), F32),
        ],
        compiler_params=_cparams(("arbitrary", "arbitrary", "arbitrary")),
        name="flash_prompt",
    )(q, q, kb, vt5, tri, *lams, sg)


def _flash_sample_kernel(pt_ref, q_ref, *refs, ppc, ntok, lam_init):
    del pt_ref
    k_refs = refs[0:ppc]
    v_refs = refs[ppc:2 * ppc]
    kn_ref, vn_ref, lq1_ref, lk1_ref, lq2_ref, lk2_ref, sg_ref, o_ref, m_ref, l_ref, acc_ref = refs[2 * ppc:]
    c = pl.program_id(1)
    rows = q_ref.shape[1]
    kvh = q_ref.shape[2] // LANES
    rpk = rows // kvh

    @pl.when(c == 0)
    def _():
        m_ref[...] = jnp.full(m_ref.shape, -jnp.inf, F32)
        l_ref[...] = jnp.zeros(l_ref.shape, F32)
        acc_ref[...] = jnp.zeros(acc_ref.shape, F32)

    q = q_ref[0]

    def update(s_list, v_list):
        s = jnp.concatenate(s_list, axis=1) if len(s_list) > 1 else s_list[0]
        m_old = m_ref[...]
        m_new = jnp.maximum(m_old, jnp.max(s, axis=1, keepdims=True))
        alpha = jnp.exp2(m_old - m_new)
        p = jnp.exp2(s - m_new)
        l_ref[...] = alpha * l_ref[...] + jnp.sum(p, axis=1, keepdims=True)
        pb = p.astype(BF16)
        pv = None
        for i, v in enumerate(v_list):
            d = jnp.dot(pb[:, i * PAGE_SIZE:(i + 1) * PAGE_SIZE], v, preferred_element_type=F32)
            pv = d if pv is None else pv + d
        acc_ref[...] = alpha * acc_ref[...] + pv
        m_ref[...] = m_new

    def page(ref):
        heads = [ref[0, 0, pl.ds(h, PAGE_SIZE, stride=kvh), :] for h in range(kvh)]
        return jnp.concatenate(heads, axis=1).astype(BF16)

    nt_dims = (((1,), (1,)), ((), ()))
    s_list = [lax.dot_general(q, page(k_refs[i]), nt_dims, preferred_element_type=F32) for i in range(ppc)]
    update(s_list, [page(v_refs[i]) for i in range(ppc)])

    @pl.when(c == pl.num_programs(1) - 1)
    def _():
        s = lax.dot_general(q, kn_ref[0], nt_dims, preferred_element_type=F32)
        key = lax.broadcasted_iota(jnp.int32, s.shape, 1)
        tok = lax.rem(lax.broadcasted_iota(jnp.int32, s.shape, 0), ntok)
        update([jnp.where(key <= tok, s, -jnp.inf)], [vn_ref[0]])
        lam = _diff_lambda(lq1_ref, lk1_ref, lq2_ref, lk2_ref, lam_init)
        half = rpk // 2
        for h in range(kvh):
            blk = (acc_ref[h * rpk:(h + 1) * rpk, h * LANES:(h + 1) * LANES]
                   / l_ref[h * rpk:(h + 1) * rpk, :])
            d = blk[0:half] - lam * blk[half:rpk]
            dn = d * lax.rsqrt(jnp.mean(d * d, axis=-1, keepdims=True) + RMS_EPS)
            o_ref[0, h * half:(h + 1) * half, :] = (dn * sg_ref[...]) * (1.0 - lam_init)


def _flash_sample(page_table, qblk, cache_k4, cache_v4, layer, knew, vnew, lams, sg, *, ppc, ntok, lam_init):
    nseq, rows, kw = qblk.shape
    n_pages = page_table.shape[1]
    nch = n_pages // ppc
    kvh = kw // LANES
    kern = functools.partial(_flash_sample_kernel, ppc=ppc, ntok=ntok, lam_init=lam_init)

    def page_spec(i):
        return pl.BlockSpec((1, 1, PAGE_SIZE * kvh, LANES), lambda n, c, pt: (layer, pt[n, c * ppc + i], 0, 0))

    seq_spec = pl.BlockSpec((1, PAGE_SIZE, kw), lambda n, c, pt: (n, 0, 0))
    vec = pl.BlockSpec((1, HEAD_DIM), lambda n, c, pt: (0, 0))
    out_rows = rows // 2
    grid_spec = pltpu.PrefetchScalarGridSpec(
        num_scalar_prefetch=1,
        grid=(nseq, nch),
        in_specs=([pl.BlockSpec((1, rows, kw), lambda n, c, pt: (n, 0, 0))]
                  + [page_spec(i) for i in range(ppc)] + [page_spec(i) for i in range(ppc)]
                  + [seq_spec, seq_spec, vec, vec, vec, vec,
                     pl.BlockSpec((1, VAL_DIM), lambda n, c, pt: (0, 0))]),
        out_specs=pl.BlockSpec((1, out_rows, VAL_DIM), lambda n, c, pt: (n, 0, 0)),
        scratch_shapes=[
            pltpu.VMEM((rows, 1), F32),
            pltpu.VMEM((rows, 1), F32),
            pltpu.VMEM((rows, kw), F32),
        ],
    )
    return pl.pallas_call(
        kern,
        grid_spec=grid_spec,
        out_shape=jax.ShapeDtypeStruct((nseq, out_rows, VAL_DIM), F32),
        compiler_params=_cparams(("arbitrary", "arbitrary")),
        name="flash_sample",
    )(page_table, qblk, *([cache_k4] * ppc), *([cache_v4] * ppc), knew, vnew, *lams, sg)


def _attn_out_kernel(o_ref, gs_ref, w_ref, x_ref, *rest, final):
    if final:
        fg_ref, y_ref = rest
    else:
        (y_ref,) = rest
    y = jnp.dot(o_ref[...] * gs_ref[...], w_ref[...], preferred_element_type=F32) + x_ref[...]
    if final:
        y = _rms(y, fg_ref[...])
    y_ref[...] = y


def _attn_out(o, gs, w, x, fg, *, tm):
    n, d = x.shape
    e = w.shape[0]
    final = fg is not None
    row = lambda wd: pl.BlockSpec((tm, wd), lambda i: (i, 0))
    in_specs = [row(e), row(e), _const_spec(w.shape), row(d)]
    args = [o, gs, w, x]
    if final:
        in_specs.append(_const_spec((1, d)))
        args.append(fg)
    return pl.pallas_call(
        functools.partial(_attn_out_kernel, final=final),
        grid=(n // tm,),
        in_specs=in_specs,
        out_specs=row(d),
        out_shape=jax.ShapeDtypeStruct((n, d), F32),
        compiler_params=_cparams(("arbitrary",)),
        name="attn_out",
    )(*args)


def _final_norm_kernel(x_ref, g_ref, y_ref):
    y_ref[...] = _rms(x_ref[...], g_ref[...])


def _final_norm(x, g, *, tm):
    n, d = x.shape
    row = pl.BlockSpec((tm, d), lambda i: (i, 0))
    return pl.pallas_call(
        _final_norm_kernel, grid=(n // tm,), in_specs=[row, _const_spec((1, d))], out_specs=row,
        out_shape=jax.ShapeDtypeStruct((n, d), F32), compiler_params=_cparams(("arbitrary",)),
        name="final_norm",
    )(x, g)


def _rope_tables(pos):
    half = ROT_DIM // 2
    inv_freq = ROPE_THETA ** (-jnp.arange(half, dtype=F32) / half)
    ang = pos.astype(F32)[:, None] * inv_freq[None, :]
    cos, sin = jnp.cos(ang), jnp.sin(ang)
    n = pos.shape[0]
    pad = jnp.zeros((n, HEAD_DIM - ROT_DIM), F32)
    zero = jnp.zeros((n, half), F32)
    c64 = jnp.concatenate([cos, cos, pad + 1.0], axis=1)
    up64 = jnp.concatenate([-sin, zero, pad], axis=1)
    dn64 = jnp.concatenate([zero, sin, pad], axis=1)
    rep = lambda t: jnp.concatenate([t, t], axis=1)
    return rep(c64), rep(up64), rep(dn64)


def _pick_tile(n, pref):
    t = min(n, pref)
    while n % t:
        t //= 2
    return t


def kernel(x_prompt, x_sample, state_conv, cache_k, cache_v, page_table, norm_g, final_norm_g,
           conv_w_in, conv_dw_w, conv_dw_b, conv_ln_g, conv_ln_b, conv_w_out,
           attn_w_in, attn_lambda_q1, attn_lambda_k1, attn_lambda_q2, attn_lambda_k2,
           attn_subln_g, attn_w_out):
    batch, seq, d = x_prompt.shape
    nseq, ntok, _ = x_sample.shape
    depth = norm_g.shape[0]
    e = conv_w_out.shape[1]
    n_heads = e // VAL_DIM
    kvh = n_heads // GROUP
    qw = n_heads * 2 * HEAD_DIM
    kvw = kvh * 2 * HEAD_DIM
    n_chunks = e // LANES
    n_pages = page_table.shape[1]
    past = n_pages * PAGE_SIZE

    tm_conv = _pick_tile(seq, 512)
    tm_proj = _pick_tile(seq, 512)
    tq = _pick_tile(seq, 512)
    ct = 256
    ppc = _pick_tile(n_pages, 16)

    yp = x_prompt.reshape(batch * seq, d)
    ys = jnp.swapaxes(x_sample, 0, 1).reshape(ntok * nseq, d)
    ns = ntok * nseq

    tabs_p = _rope_tables(jnp.arange(seq, dtype=jnp.int32))
    tabs_s = _rope_tables(past + jnp.repeat(jnp.arange(ntok, dtype=jnp.int32), nseq))

    cache_k4 = cache_k.reshape(cache_k.shape[0], cache_k.shape[1], PAGE_SIZE * kvh, 2 * HEAD_DIM)
    cache_v4 = cache_v.reshape(cache_v.shape[0], cache_v.shape[1], PAGE_SIZE * kvh, VAL_DIM)
    eye_h = jnp.eye(kvh, dtype=BF16)
    eye_c = jnp.eye(2, dtype=BF16)

    n_attn = depth // N_MIXERS
    conv_p, conv_s = [], []
    kv_p = kv_s = None
    for i in range(depth):
        j = i // N_MIXERS
        ng = norm_g[i].reshape(1, d)
        last = i == depth - 1
        if i % N_MIXERS == 0:
            win = conv_w_in[j].astype(BF16)
            wout = conv_w_out[j].astype(BF16)
            dww3 = jnp.pad(conv_dw_w[j], ((0, HALO - CONV_WIDTH), (0, 0))).reshape(HALO, n_chunks, LANES)
            dww3 = jnp.swapaxes(dww3, 0, 1)
            dwb3 = conv_dw_b[j].reshape(n_chunks, 1, LANES)
            lng = conv_ln_g[j].reshape(1, e)
            lnb = conv_ln_b[j].reshape(1, e)
            yp, sp = _conv_prompt(yp, ng, win, dww3, dwb3, lng, lnb, wout, batch=batch, seq=seq, tm=tm_conv)
            prev_t = jnp.swapaxes(state_conv[j], 0, 1)
            ys, ss_t = _conv_sample(ys, ng, win, prev_t, dww3, dwb3, lng, lnb, wout, nseq=nseq, ntok=ntok)
            conv_p.append(sp)
            conv_s.append(jnp.swapaxes(ss_t, 0, 1))
            if last:
                yp = _final_norm(yp, final_norm_g.reshape(1, d), tm=tm_proj)
                ys = _final_norm(ys, final_norm_g.reshape(1, d), tm=ns)
        else:
            lam_init = 0.8 - 0.6 * math.exp(-0.3 * i)
            win = attn_w_in[j].astype(BF16)
            wout = attn_w_out[j].astype(BF16)
            lams = [a[j].reshape(1, HEAD_DIM) for a in
                    (attn_lambda_q1, attn_lambda_k1, attn_lambda_q2, attn_lambda_k2)]
            sg = attn_subln_g[j].reshape(1, VAL_DIM)
            fg = final_norm_g.reshape(1, d) if last else None

            q, kf, kb, vf, vt5, gs = _attn_proj(yp, ng, win, *tabs_p, kv_p, layer=j, n_layers=n_attn, qw=qw,
                                                kvw=kvw, tm=tm_proj, seq_tiles=seq // tm_proj, transpose_v=True)
            kv_p = (kf, vf)
            o = _flash_prompt(q, kb, vt5, lams, sg, batch=batch, seq=seq, tq=tq, ct=ct, lam_init=lam_init)
            yp = _attn_out(o, gs, wout, yp, fg, tm=tm_proj)

            q, kf, kb, vf, vb, gs = _attn_proj(ys, ng, win, *tabs_s, kv_s, layer=j, n_layers=n_attn, qw=qw,
                                               kvw=kvw, tm=ns, seq_tiles=1, transpose_v=False)
            kv_s = (kf, vf)
            q6 = q.reshape(ntok, nseq, kvh, GROUP, 2, HEAD_DIM).transpose(1, 2, 4, 3, 0, 5)
            q6 = q6.reshape(nseq, kvh, 2, GROUP * ntok, HEAD_DIM)
            qblk = (q6[:, :, :, :, None, None, :] * eye_h[None, :, None, None, :, None, None]
                    * eye_c[None, None, :, None, None, :, None])
            qblk = qblk.reshape(nseq, kvh * 2 * GROUP * ntok, kvw)
            pad_new = lambda a: jnp.pad(jnp.swapaxes(a.reshape(ntok, nseq, kvw), 0, 1),
                                        ((0, 0), (0, PAGE_SIZE - ntok), (0, 0)))
            od = _flash_sample(page_table, qblk, cache_k4, cache_v4, j, pad_new(kb), pad_new(vb), lams, sg,
                               ppc=ppc, ntok=ntok, lam_init=lam_init)
            o = od.reshape(nseq, kvh, GROUP, ntok, VAL_DIM).transpose(3, 0, 1, 2, 4).reshape(ns, e)
            ys = _attn_out(o.astype(BF16), gs, wout, ys, fg, tm=ns)

    y_prompt = yp.reshape(batch, seq, d)
    y_sample = jnp.swapaxes(ys.reshape(ntok, nseq, d), 0, 1)
    rows_p = lambda a: a.reshape(n_attn, batch, seq, kvh, VAL_DIM)
    rows_s = lambda a: jnp.swapaxes(a.reshape(n_attn, ntok, nseq, kvh, VAL_DIM), 1, 2)
    return (y_prompt, y_sample, jnp.stack(conv_p), jnp.stack(conv_s),
            rows_p(kv_p[0]), rows_p(kv_p[1]), rows_s(kv_s[0]), rows_s(kv_s[1]))
```

```python
import functools
import math

import jax
import jax.numpy as jnp
from jax import lax
from jax.experimental import pallas as pl
from jax.experimental.pallas import tpu as pltpu

F32 = jnp.float32
BF16 = jnp.bfloat16

N_MIXERS = 2
CONV_WIDTH = 31
HEAD_DIM = 64
VAL_DIM = 2 * HEAD_DIM
GROUP = 4
ROT_DIM = HEAD_DIM // 4
ROPE_THETA = 500000.0
RMS_EPS = 1e-6
LN_EPS = 1e-5
SCALE = HEAD_DIM ** -0.5
Q_SCALE = SCALE * math.log2(math.e)
PAGE_SIZE = 128

LANES = 128
HALO = 32
VMEM_LIMIT = 56 * 1024 * 1024


def _cparams(sem):
    return pltpu.CompilerParams(dimension_semantics=sem, vmem_limit_bytes=VMEM_LIMIT)


def _const_spec(shape, single=False):
    nd = len(shape)
    return pl.BlockSpec(shape, lambda *_: (0,) * nd, pipeline_mode=pl.Buffered(1) if single else None)


def _rms(x, g):
    return x * lax.rsqrt(jnp.mean(x * x, axis=-1, keepdims=True) + RMS_EPS) * g


def _sigmoid(x):
    return 1.0 / (1.0 + jnp.exp(-x))


def _conv_tail(cb_ref, m_ref, gate, lng_ref, lnb_ref, n_chunks):
    tot = cb_ref[0]
    for c in range(1, n_chunks):
        tot = tot + cb_ref[c]
    width = n_chunks * LANES
    mu = jnp.sum(tot, axis=-1, keepdims=True) / width
    sq = None
    for c in range(n_chunks):
        d = cb_ref[c] - mu
        sq = d * d if sq is None else sq + d * d
    rstd = lax.rsqrt(jnp.sum(sq, axis=-1, keepdims=True) / width + LN_EPS)
    for c in range(n_chunks):
        sl = slice(c * LANES, (c + 1) * LANES)
        cn = (cb_ref[c] - mu) * rstd * lng_ref[:, sl] + lnb_ref[:, sl]
        act = cn * _sigmoid(cn)
        g = gate[:, sl]
        m_ref[:, sl] = (act * (g * _sigmoid(g))).astype(BF16)


def _conv_prompt_kernel(x_ref, ng_ref, win_ref, dww_ref, dwb_ref, lng_ref, lnb_ref, wout_ref,
                        y_ref, st_ref, upad_ref, cb_ref, m_ref, *, tm, nt, e):
    t = pl.program_id(1)
    n_chunks = e // LANES

    @pl.when(t == 0)
    def _():
        upad_ref[:, 0:HALO, :] = jnp.zeros((n_chunks, HALO, LANES), F32)

    x = x_ref[...]
    h = _rms(x, ng_ref[...]).astype(BF16)
    a = jnp.dot(h, win_ref[:, 0:e], preferred_element_type=F32)
    b = jnp.dot(h, win_ref[:, e:2 * e], preferred_element_type=F32)
    u = a * _sigmoid(b)
    for c in range(n_chunks):
        upad_ref[c, HALO:HALO + tm, :] = u[:, c * LANES:(c + 1) * LANES]

    off = HALO - (CONV_WIDTH - 1)

    def chunk(c, carry):
        acc = jnp.broadcast_to(dwb_ref[c], (tm, LANES))
        for k in range(CONV_WIDTH):
            acc = acc + dww_ref[c, k:k + 1, :] * upad_ref[c, off + k:off + k + tm, :]
        cb_ref[c] = acc
        return carry

    lax.fori_loop(0, n_chunks, chunk, 0)

    gate = jnp.dot(h, win_ref[:, 2 * e:3 * e], preferred_element_type=F32)
    _conv_tail(cb_ref, m_ref, gate, lng_ref, lnb_ref, n_chunks)
    y_ref[...] = jnp.dot(m_ref[...], wout_ref[...], preferred_element_type=F32) + x

    @pl.when(t == nt - 1)
    def _():
        for c in range(n_chunks):
            st_ref[0, :, c * LANES:(c + 1) * LANES] = upad_ref[c, tm + off:tm + HALO, :]

    upad_ref[:, 0:HALO, :] = upad_ref[:, tm:tm + HALO, :]


def _conv_prompt(x, ng, win, dww3, dwb3, lng, lnb, wout, *, batch, seq, tm):
    n, d = x.shape
    e = wout.shape[0]
    nt = seq // tm
    n_chunks = e // LANES
    kern = functools.partial(_conv_prompt_kernel, tm=tm, nt=nt, e=e)
    return pl.pallas_call(
        kern,
        grid=(batch, nt),
        in_specs=[
            pl.BlockSpec((tm, d), lambda b, t: (b * nt + t, 0)),
            _const_spec((1, d)),
            _const_spec(win.shape, single=True),
            _const_spec(dww3.shape),
            _const_spec(dwb3.shape),
            _const_spec((1, e)),
            _const_spec((1, e)),
            _const_spec(wout.shape, single=True),
        ],
        out_specs=[
            pl.BlockSpec((tm, d), lambda b, t: (b * nt + t, 0)),
            pl.BlockSpec((1, CONV_WIDTH - 1, e), lambda b, t: (b, 0, 0)),
        ],
        out_shape=[
            jax.ShapeDtypeStruct((n, d), F32),
            jax.ShapeDtypeStruct((batch, CONV_WIDTH - 1, e), F32),
        ],
        scratch_shapes=[
            pltpu.VMEM((n_chunks, HALO + tm, LANES), F32),
            pltpu.VMEM((n_chunks, tm, LANES), F32),
            pltpu.VMEM((tm, e), BF16),
        ],
        compiler_params=_cparams(("arbitrary", "arbitrary")),
        name="conv_prompt",
    )(x, ng, win, dww3, dwb3, lng, lnb, wout)


def _conv_sample_kernel(x_ref, ng_ref, win_ref, prev_ref, dww_ref, dwb_ref, lng_ref, lnb_ref, wout_ref,
                        y_ref, st_ref, u_ref, cb_ref, m_ref, *, nseq, ntok, e):
    n_chunks = e // LANES
    hist = CONV_WIDTH - 1
    x = x_ref[...]
    h = _rms(x, ng_ref[...]).astype(BF16)
    a = jnp.dot(h, win_ref[:, 0:e], preferred_element_type=F32)
    b = jnp.dot(h, win_ref[:, e:2 * e], preferred_element_type=F32)
    u_ref[...] = a * _sigmoid(b)

    def row(r, sl):
        if r < hist:
            return prev_ref[r, :, sl]
        return u_ref[(r - hist) * nseq:(r - hist + 1) * nseq, sl]

    for c in range(n_chunks):
        sl = slice(c * LANES, (c + 1) * LANES)
        for t in range(ntok):
            acc = jnp.broadcast_to(dwb_ref[c], (nseq, LANES))
            for k in range(CONV_WIDTH):
                acc = acc + dww_ref[c, k:k + 1, :] * row(t + k, sl)
            cb_ref[c, t * nseq:(t + 1) * nseq, :] = acc
        for r in range(hist):
            st_ref[r, :, sl] = row(r + ntok, sl)

    gate = jnp.dot(h, win_ref[:, 2 * e:3 * e], preferred_element_type=F32)
    _conv_tail(cb_ref, m_ref, gate, lng_ref, lnb_ref, n_chunks)
    y_ref[...] = jnp.dot(m_ref[...], wout_ref[...], preferred_element_type=F32) + x


def _conv_sample(x, ng, win, prev_t, dww3, dwb3, lng, lnb, wout, *, nseq, ntok):
    n, d = x.shape
    e = wout.shape[0]
    n_chunks = e // LANES
    kern = functools.partial(_conv_sample_kernel, nseq=nseq, ntok=ntok, e=e)
    args = (x, ng, win, prev_t, dww3, dwb3, lng, lnb, wout)
    return pl.pallas_call(
        kern,
        grid=(1,),
        in_specs=[_const_spec(a.shape) for a in args],
        out_specs=[_const_spec((n, d)), _const_spec(prev_t.shape)],
        out_shape=[jax.ShapeDtypeStruct((n, d), F32), jax.ShapeDtypeStruct(prev_t.shape, F32)],
        scratch_shapes=[
            pltpu.VMEM((n, e), F32),
            pltpu.VMEM((n_chunks, n, LANES), F32),
            pltpu.VMEM((n, e), BF16),
        ],
        compiler_params=_cparams(("arbitrary",)),
        name="conv_sample",
    )(*args)


def _rope_tile(x, cos, sin_up, sin_dn):
    return (x * cos + pltpu.roll(x, LANES - ROT_DIM // 2, axis=1) * sin_up
            + pltpu.roll(x, ROT_DIM // 2, axis=1) * sin_dn)


def _attn_proj_kernel(x_ref, ng_ref, win_ref, cos_ref, sup_ref, sdn_ref, *rest, qw, kvw, transpose_v, layer):
    q_ref, kf_ref, kb_ref, vf_ref, vb_ref, gs_ref = rest[-6:]
    if layer:
        kf_ref[0:layer] = rest[0][...]
        vf_ref[0:layer] = rest[1][...]
    kvh = kvw // LANES
    tm = x_ref.shape[0]
    x = x_ref[...]
    h = _rms(x, ng_ref[...]).astype(BF16)
    cos, sup, sdn = cos_ref[...], sup_ref[...], sdn_ref[...]
    q = jnp.dot(h, win_ref[:, 0:qw], preferred_element_type=F32)
    for j in range(qw // LANES):
        sl = slice(j * LANES, (j + 1) * LANES)
        q_ref[:, sl] = (_rope_tile(q[:, sl], cos, sup, sdn) * Q_SCALE).astype(BF16)
    k = jnp.dot(h, win_ref[:, qw:qw + kvw], preferred_element_type=F32)
    for j in range(kvh):
        sl = slice(j * LANES, (j + 1) * LANES)
        kr = _rope_tile(k[:, sl], cos, sup, sdn)
        kf_ref[layer, pl.ds(j, tm, stride=kvh), :] = kr
        kb_ref[:, sl] = kr.astype(BF16)
    v = jnp.dot(h, win_ref[:, qw + kvw:qw + 2 * kvw], preferred_element_type=F32)
    for j in range(kvh):
        vf_ref[layer, pl.ds(j, tm, stride=kvh), :] = v[:, j * LANES:(j + 1) * LANES]
    if transpose_v:
        vt = v.T.astype(BF16)
        for j in range(kvh):
            vb_ref[0, 0, j] = vt[j * LANES:(j + 1) * LANES, :]
    else:
        vb_ref[...] = v.astype(BF16)
    g = jnp.dot(h, win_ref[:, qw + 2 * kvw:], preferred_element_type=F32)
    gs_ref[...] = (g * _sigmoid(g)).astype(BF16)


def _attn_proj(x, ng, win, cos, sup, sdn, kv_rows, *, layer, qw, kvw, tm, seq_tiles, transpose_v):
    n, d = x.shape
    nt = n // tm
    kvh = kvw // LANES
    gw = win.shape[1] - qw - 2 * kvw
    row = lambda w: pl.BlockSpec((tm, w), lambda i: (i, 0))
    tab = pl.BlockSpec((tm, LANES), lambda i: (i % seq_tiles, 0))
    if transpose_v:
        vb_spec = pl.BlockSpec((1, 1, kvh, LANES, tm), lambda i: (i // seq_tiles, i % seq_tiles, 0, 0, 0))
        vb_shape = jax.ShapeDtypeStruct((nt // seq_tiles, seq_tiles, kvh, LANES, tm), BF16)
    else:
        vb_spec = row(kvw)
        vb_shape = jax.ShapeDtypeStruct((n, kvw), BF16)
    assert (kv_rows is None) == (layer == 0)
    kern = functools.partial(_attn_proj_kernel, qw=qw, kvw=kvw, transpose_v=transpose_v, layer=layer)
    rows_spec = pl.BlockSpec((layer + 1, tm * kvh, LANES), lambda i: (0, i, 0))
    rows_shape = jax.ShapeDtypeStruct((layer + 1, n * kvh, LANES), F32)
    in_specs = [row(d), _const_spec((1, d)), _const_spec(win.shape), tab, tab, tab]
    args = [x, ng, win, cos, sup, sdn]
    if layer:
        in_specs += [pl.BlockSpec((layer, tm * kvh, LANES), lambda i: (0, i, 0))] * 2
        args += list(kv_rows)
    return pl.pallas_call(
        kern,
        grid=(nt,),
        in_specs=in_specs,
        out_specs=[row(qw), rows_spec, row(kvw), rows_spec, vb_spec, row(gw)],
        out_shape=[
            jax.ShapeDtypeStruct((n, qw), BF16),
            rows_shape,
            jax.ShapeDtypeStruct((n, kvw), BF16),
            rows_shape,
            vb_shape,
            jax.ShapeDtypeStruct((n, gw), BF16),
        ],
        compiler_params=_cparams(("arbitrary",)),
        name="attn_proj",
    )(*args)


def _diff_lambda(lq1_ref, lk1_ref, lq2_ref, lk2_ref, lam_init):
    s1 = jnp.sum(lq1_ref[...] * lk1_ref[...], axis=-1, keepdims=True)
    s2 = jnp.sum(lq2_ref[...] * lk2_ref[...], axis=-1, keepdims=True)
    return jnp.exp(s1) - jnp.exp(s2) + lam_init


def _flash_prompt_kernel(q_ref, qn_ref, k_ref, vt_ref, tri_ref, lq1_ref, lk1_ref, lq2_ref, lk2_ref, sg_ref, o_ref,
                         qp_ref, m_ref, l_ref, acc_ref, s0_ref, s1_ref, s2_ref, x0_ref, x1_ref, x2_ref,
                         *, tq, sb, ct, lam_init):
    qi = pl.program_id(2)
    r = 2 * GROUP * tq
    n_ct = r // ct
    nt_dims = (((1,), (1,)), ((), ()))

    def pad_queries(src_ref):
        lane = lax.broadcasted_iota(jnp.int32, (tq, LANES), 1)
        zero = jnp.zeros((tq, LANES), BF16)
        for g in range(GROUP):
            qg = src_ref[:, g * LANES:(g + 1) * LANES]
            qp_ref[(2 * g) * tq:(2 * g + 1) * tq, :] = jnp.where(lane < HEAD_DIM, qg, zero)
            qp_ref[(2 * g + 1) * tq:(2 * g + 2) * tq, :] = jnp.where(lane >= HEAD_DIM, qg, zero)

    def step(nxt, cur, nxt_diag=False):
        if nxt is not None:
            sn_ref, xn_ref, jn = nxt
        if cur is not None:
            sc_ref, xc_ref, jc, diag = cur
        for c in range(n_ct):
            cs = slice(c * ct, (c + 1) * ct)
            t0 = (c * ct) % tq
            if nxt is not None:
                rows = t0 + ct if nxt_diag else sb
                kb = k_ref[pl.ds(pl.multiple_of(jn * sb, sb), rows), :]
                sn = lax.dot_general(kb, qp_ref[cs, :], nt_dims, preferred_element_type=F32)
                sn_ref[0:rows, cs] = sn
                if not nxt_diag:
                    xn_ref[:, cs] = jnp.max(sn, axis=0, keepdims=True)
            if cur is None:
                continue
            if diag:
                rows = t0 + ct
                s = sc_ref[t0:rows, cs] + tri_ref[...]
                if t0:
                    s = jnp.concatenate([sc_ref[0:t0, cs], s], axis=0)
                mx = jnp.max(s, axis=0, keepdims=True)
            else:
                rows = sb
                s = sc_ref[:, cs]
                mx = xc_ref[:, cs]
            vt = vt_ref[0, jc, 0, :, 0:rows]
            m_old = m_ref[:, cs]
            m_new = jnp.maximum(m_old, mx)
            alpha = jnp.exp2(m_old - m_new)
            p = jnp.exp2(s - m_new)
            l_ref[:, cs] = alpha * l_ref[:, cs] + jnp.sum(p, axis=0, keepdims=True)
            pv = jnp.dot(vt, p.astype(BF16), preferred_element_type=F32)
            acc_ref[:, cs] = alpha * acc_ref[:, cs] + pv
            m_ref[:, cs] = m_new

    b0 = (s0_ref, x0_ref)
    b1 = (s1_ref, x1_ref)
    b2 = (s2_ref, x2_ref)

    @pl.when(qi == 0)
    def _():
        pad_queries(q_ref)
        step(b2 + (0,), None)

    m_ref[...] = jnp.full((1, r), -jnp.inf, F32)
    l_ref[...] = jnp.zeros((1, r), F32)
    acc_ref[...] = jnp.zeros((VAL_DIM, r), F32)

    nv = (qi * tq) // sb

    @pl.when(nv >= 1)
    def _():
        step(b1 + (1,), b2 + (0, False))

    rest = jnp.maximum(nv - 1, 0)

    def pair(jj, carry):
        j0 = 1 + 2 * jj
        step(b0 + (j0 + 1,), b1 + (j0, False))
        step(b1 + (j0 + 2,), b0 + (j0 + 1, False))
        return carry

    lax.fori_loop(0, rest // 2, pair, 0)
    rest_odd = lax.rem(rest, 2) == 1

    @pl.when(nv == 0)
    def _():
        step(None, b2 + (0, True))
        pad_queries(qn_ref)
        step(b2 + (0,), None)

    @pl.when(jnp.logical_and(nv >= 1, jnp.logical_not(rest_odd)))
    def _():
        pad_queries(qn_ref)
        step(b2 + (0,), b1 + (nv, True))

    @pl.when(jnp.logical_and(nv >= 1, rest_odd))
    def _():
        step(b0 + (nv,), b1 + (nv - 1, False), nxt_diag=True)
        pad_queries(qn_ref)
        step(b2 + (0,), b0 + (nv, True))

    lam = _diff_lambda(lq1_ref, lk1_ref, lq2_ref, lk2_ref, lam_init)
    for g in range(GROUP):
        s1 = slice((2 * g) * tq, (2 * g + 1) * tq)
        s2 = slice((2 * g + 1) * tq, (2 * g + 2) * tq)
        o1 = acc_ref[:, s1] * (1.0 / l_ref[:, s1])
        o2 = acc_ref[:, s2] * (1.0 / l_ref[:, s2])
        d = o1 - lam * o2
        dn = d * lax.rsqrt(jnp.mean(d * d, axis=0, keepdims=True) + RMS_EPS)
        o_ref[:, g * LANES:(g + 1) * LANES] = ((dn.T * sg_ref[...]) * (1.0 - lam_init)).astype(BF16)


def _flash_prompt(q, kb, vt5, lams, sg, *, batch, seq, tq, ct, lam_init):
    n, qw = q.shape
    kvh = kb.shape[1] // LANES
    sb = vt5.shape[-1]
    nq = seq // tq
    r = 2 * GROUP * tq
    assert tq == sb and tq % ct == 0, (tq, sb, ct)
    kern = functools.partial(_flash_prompt_kernel, tq=tq, sb=sb, ct=ct, lam_init=lam_init)
    vec = _const_spec((1, HEAD_DIM))
    tri = jnp.where(lax.broadcasted_iota(jnp.int32, (ct, ct), 0) <= lax.broadcasted_iota(jnp.int32, (ct, ct), 1),
                    0.0, -jnp.inf).astype(F32)
    return pl.pallas_call(
        kern,
        grid=(batch, kvh, nq),
        in_specs=[
            pl.BlockSpec((tq, GROUP * LANES), lambda b, h, i: (b * nq + i, h)),
            pl.BlockSpec((tq, GROUP * LANES), lambda b, h, i: (b * nq + jnp.minimum(i + 1, nq - 1), h)),
            pl.BlockSpec((seq, LANES), lambda b, h, i: (b, h)),
            pl.BlockSpec((1, seq // sb, 1, LANES, sb), lambda b, h, i: (b, 0, h, 0, 0)),
            _const_spec((ct, ct)),
            vec, vec, vec, vec,
            _const_spec((1, VAL_DIM)),
        ],
        out_specs=pl.BlockSpec((tq, GROUP * LANES), lambda b, h, i: (b * nq + i, h)),
        out_shape=jax.ShapeDtypeStruct((n, qw), BF16),
        scratch_shapes=[
            pltpu.VMEM((r, LANES), BF16),
            pltpu.VMEM((1, r), F32),
            pltpu.VMEM((1, r), F32),
            pltpu.VMEM((VAL_DIM, r), F32),
            pltpu.VMEM((sb, r), F32),
            pltpu.VMEM((sb, r), F32),
            pltpu.VMEM((sb, r), F32),
            pltpu.VMEM((1, r), F32),
            pltpu.VMEM((1, r), F32),
            pltpu.VMEM((1, r), F32),
        ],
        compiler_params=_cparams(("arbitrary", "arbitrary", "arbitrary")),
        name="flash_prompt",
    )(q, q, kb, vt5, tri, *lams, sg)


def _flash_sample_kernel(pt_ref, q_ref, *refs, ppc, ntok, lam_init):
    del pt_ref
    k_refs = refs[0:ppc]
    v_refs = refs[ppc:2 * ppc]
    kn_ref, vn_ref, lq1_ref, lk1_ref, lq2_ref, lk2_ref, sg_ref, o_ref, m_ref, l_ref, acc_ref = refs[2 * ppc:]
    c = pl.program_id(1)
    rows = q_ref.shape[1]
    kvh = q_ref.shape[2] // LANES
    rpk = rows // kvh

    @pl.when(c == 0)
    def _():
        m_ref[...] = jnp.full(m_ref.shape, -jnp.inf, F32)
        l_ref[...] = jnp.zeros(l_ref.shape, F32)
        acc_ref[...] = jnp.zeros(acc_ref.shape, F32)

    q = q_ref[0]

    def update(s_list, v_list):
        s = jnp.concatenate(s_list, axis=1) if len(s_list) > 1 else s_list[0]
        m_old = m_ref[...]
        m_new = jnp.maximum(m_old, jnp.max(s, axis=1, keepdims=True))
        alpha = jnp.exp2(m_old - m_new)
        p = jnp.exp2(s - m_new)
        l_ref[...] = alpha * l_ref[...] + jnp.sum(p, axis=1, keepdims=True)
        pb = p.astype(BF16)
        pv = None
        for i, v in enumerate(v_list):
            d = jnp.dot(pb[:, i * PAGE_SIZE:(i + 1) * PAGE_SIZE], v, preferred_element_type=F32)
            pv = d if pv is None else pv + d
        acc_ref[...] = alpha * acc_ref[...] + pv
        m_ref[...] = m_new

    def page(ref):
        heads = [ref[0, 0, pl.ds(h, PAGE_SIZE, stride=kvh), :] for h in range(kvh)]
        return jnp.concatenate(heads, axis=1).astype(BF16)

    nt_dims = (((1,), (1,)), ((), ()))
    s_list = [lax.dot_general(q, page(k_refs[i]), nt_dims, preferred_element_type=F32) for i in range(ppc)]
    update(s_list, [page(v_refs[i]) for i in range(ppc)])

    @pl.when(c == pl.num_programs(1) - 1)
    def _():
        s = lax.dot_general(q, kn_ref[0], nt_dims, preferred_element_type=F32)
        key = lax.broadcasted_iota(jnp.int32, s.shape, 1)
        tok = lax.rem(lax.broadcasted_iota(jnp.int32, s.shape, 0), ntok)
        update([jnp.where(key <= tok, s, -jnp.inf)], [vn_ref[0]])
        lam = _diff_lambda(lq1_ref, lk1_ref, lq2_ref, lk2_ref, lam_init)
        half = rpk // 2
        for h in range(kvh):
            blk = (acc_ref[h * rpk:(h + 1) * rpk, h * LANES:(h + 1) * LANES]
                   / l_ref[h * rpk:(h + 1) * rpk, :])
            d = blk[0:half] - lam * blk[half:rpk]
            dn = d * lax.rsqrt(jnp.mean(d * d, axis=-1, keepdims=True) + RMS_EPS)
            o_ref[0, h * half:(h + 1) * half, :] = (dn * sg_ref[...]) * (1.0 - lam_init)


def _flash_sample(page_table, qblk, cache_k4, cache_v4, layer, knew, vnew, lams, sg, *, ppc, ntok, lam_init):
    nseq, rows, kw = qblk.shape
    n_pages = page_table.shape[1]
    nch = n_pages // ppc
    kvh = kw // LANES
    kern = functools.partial(_flash_sample_kernel, ppc=ppc, ntok=ntok, lam_init=lam_init)

    def page_spec(i):
        return pl.BlockSpec((1, 1, PAGE_SIZE * kvh, LANES), lambda n, c, pt: (layer, pt[n, c * ppc + i], 0, 0))

    seq_spec = pl.BlockSpec((1, PAGE_SIZE, kw), lambda n, c, pt: (n, 0, 0))
    vec = pl.BlockSpec((1, HEAD_DIM), lambda n, c, pt: (0, 0))
    out_rows = rows // 2
    grid_spec = pltpu.PrefetchScalarGridSpec(
        num_scalar_prefetch=1,
        grid=(nseq, nch),
        in_specs=([pl.BlockSpec((1, rows, kw), lambda n, c, pt: (n, 0, 0))]
                  + [page_spec(i) for i in range(ppc)] + [page_spec(i) for i in range(ppc)]
                  + [seq_spec, seq_spec, vec, vec, vec, vec,
                     pl.BlockSpec((1, VAL_DIM), lambda n, c, pt: (0, 0))]),
        out_specs=pl.BlockSpec((1, out_rows, VAL_DIM), lambda n, c, pt: (n, 0, 0)),
        scratch_shapes=[
            pltpu.VMEM((rows, 1), F32),
            pltpu.VMEM((rows, 1), F32),
            pltpu.VMEM((rows, kw), F32),
        ],
    )
    return pl.pallas_call(
        kern,
        grid_spec=grid_spec,
        out_shape=jax.ShapeDtypeStruct((nseq, out_rows, VAL_DIM), F32),
        compiler_params=_cparams(("arbitrary", "arbitrary")),
        name="flash_sample",
    )(page_table, qblk, *([cache_k4] * ppc), *([cache_v4] * ppc), knew, vnew, *lams, sg)


def _attn_out_kernel(o_ref, gs_ref, w_ref, x_ref, *rest, final):
    if final:
        fg_ref, y_ref = rest
    else:
        (y_ref,) = rest
    y = jnp.dot(o_ref[...] * gs_ref[...], w_ref[...], preferred_element_type=F32) + x_ref[...]
    if final:
        y = _rms(y, fg_ref[...])
    y_ref[...] = y


def _attn_out(o, gs, w, x, fg, *, tm):
    n, d = x.shape
    e = w.shape[0]
    final = fg is not None
    row = lambda wd: pl.BlockSpec((tm, wd), lambda i: (i, 0))
    in_specs = [row(e), row(e), _const_spec(w.shape), row(d)]
    args = [o, gs, w, x]
    if final:
        in_specs.append(_const_spec((1, d)))
        args.append(fg)
    return pl.pallas_call(
        functools.partial(_attn_out_kernel, final=final),
        grid=(n // tm,),
        in_specs=in_specs,
        out_specs=row(d),
        out_shape=jax.ShapeDtypeStruct((n, d), F32),
        compiler_params=_cparams(("arbitrary",)),
        name="attn_out",
    )(*args)


def _final_norm_kernel(x_ref, g_ref, y_ref):
    y_ref[...] = _rms(x_ref[...], g_ref[...])


def _final_norm(x, g, *, tm):
    n, d = x.shape
    row = pl.BlockSpec((tm, d), lambda i: (i, 0))
    return pl.pallas_call(
        _final_norm_kernel, grid=(n // tm,), in_specs=[row, _const_spec((1, d))], out_specs=row,
        out_shape=jax.ShapeDtypeStruct((n, d), F32), compiler_params=_cparams(("arbitrary",)),
        name="final_norm",
    )(x, g)


def _rope_tables(pos):
    half = ROT_DIM // 2
    inv_freq = ROPE_THETA ** (-jnp.arange(half, dtype=F32) / half)
    ang = pos.astype(F32)[:, None] * inv_freq[None, :]
    cos, sin = jnp.cos(ang), jnp.sin(ang)
    n = pos.shape[0]
    pad = jnp.zeros((n, HEAD_DIM - ROT_DIM), F32)
    zero = jnp.zeros((n, half), F32)
    c64 = jnp.concatenate([cos, cos, pad + 1.0], axis=1)
    up64 = jnp.concatenate([-sin, zero, pad], axis=1)
    dn64 = jnp.concatenate([zero, sin, pad], axis=1)
    rep = lambda t: jnp.concatenate([t, t], axis=1)
    return rep(c64), rep(up64), rep(dn64)


def _pick_tile(n, pref):
    t = min(n, pref)
    while n % t:
        t //= 2
    return t


def kernel(x_prompt, x_sample, state_conv, cache_k, cache_v, page_table, norm_g, final_norm_g,
           conv_w_in, conv_dw_w, conv_dw_b, conv_ln_g, conv_ln_b, conv_w_out,
           attn_w_in, attn_lambda_q1, attn_lambda_k1, attn_lambda_q2, attn_lambda_k2,
           attn_subln_g, attn_w_out):
    batch, seq, d = x_prompt.shape
    nseq, ntok, _ = x_sample.shape
    depth = norm_g.shape[0]
    e = conv_w_out.shape[1]
    n_heads = e // VAL_DIM
    kvh = n_heads // GROUP
    qw = n_heads * 2 * HEAD_DIM
    kvw = kvh * 2 * HEAD_DIM
    n_chunks = e // LANES
    n_pages = page_table.shape[1]
    past = n_pages * PAGE_SIZE

    tm_conv = _pick_tile(seq, 512)
    tm_proj = _pick_tile(seq, 512)
    tm_out = _pick_tile(batch * seq, 1024)
    tq = _pick_tile(seq, 512)
    ct = 256
    ppc = _pick_tile(n_pages, 16)

    yp = x_prompt.reshape(batch * seq, d)
    ys = jnp.swapaxes(x_sample, 0, 1).reshape(ntok * nseq, d)
    ns = ntok * nseq

    tabs_p = _rope_tables(jnp.arange(seq, dtype=jnp.int32))
    tabs_s = _rope_tables(past + jnp.repeat(jnp.arange(ntok, dtype=jnp.int32), nseq))

    cache_k4 = cache_k.reshape(cache_k.shape[0], cache_k.shape[1], PAGE_SIZE * kvh, 2 * HEAD_DIM)
    cache_v4 = cache_v.reshape(cache_v.shape[0], cache_v.shape[1], PAGE_SIZE * kvh, VAL_DIM)
    eye_h = jnp.eye(kvh, dtype=BF16)
    eye_c = jnp.eye(2, dtype=BF16)

    n_attn = depth // N_MIXERS
    conv_p, conv_s = [], []
    kv_p = kv_s = None
    for i in range(depth):
        j = i // N_MIXERS
        ng = norm_g[i].reshape(1, d)
        last = i == depth - 1
        if i % N_MIXERS == 0:
            win = conv_w_in[j].astype(BF16)
            wout = conv_w_out[j].astype(BF16)
            dww3 = jnp.pad(conv_dw_w[j], ((0, HALO - CONV_WIDTH), (0, 0))).reshape(HALO, n_chunks, LANES)
            dww3 = jnp.swapaxes(dww3, 0, 1)
            dwb3 = conv_dw_b[j].reshape(n_chunks, 1, LANES)
            lng = conv_ln_g[j].reshape(1, e)
            lnb = conv_ln_b[j].reshape(1, e)
            yp, sp = _conv_prompt(yp, ng, win, dww3, dwb3, lng, lnb, wout, batch=batch, seq=seq, tm=tm_conv)
            prev_t = jnp.swapaxes(state_conv[j], 0, 1)
            ys, ss_t = _conv_sample(ys, ng, win, prev_t, dww3, dwb3, lng, lnb, wout, nseq=nseq, ntok=ntok)
            conv_p.append(sp)
            conv_s.append(jnp.swapaxes(ss_t, 0, 1))
            if last:
                yp = _final_norm(yp, final_norm_g.reshape(1, d), tm=tm_proj)
                ys = _final_norm(ys, final_norm_g.reshape(1, d), tm=ns)
        else:
            lam_init = 0.8 - 0.6 * math.exp(-0.3 * i)
            win = attn_w_in[j].astype(BF16)
            wout = attn_w_out[j].astype(BF16)
            lams = [a[j].reshape(1, HEAD_DIM) for a in
                    (attn_lambda_q1, attn_lambda_k1, attn_lambda_q2, attn_lambda_k2)]
            sg = attn_subln_g[j].reshape(1, VAL_DIM)
            fg = final_norm_g.reshape(1, d) if last else None

            q, kf, kb, vf, vt5, gs = _attn_proj(yp, ng, win, *tabs_p, kv_p, layer=j, qw=qw,
                                                kvw=kvw, tm=tm_proj, seq_tiles=seq // tm_proj, transpose_v=True)
            kv_p = (kf, vf)
            o = _flash_prompt(q, kb, vt5, lams, sg, batch=batch, seq=seq, tq=tq, ct=ct, lam_init=lam_init)
            yp = _attn_out(o, gs, wout, yp, fg, tm=tm_out)

            q, kf, kb, vf, vb, gs = _attn_proj(ys, ng, win, *tabs_s, kv_s, layer=j, qw=qw,
                                               kvw=kvw, tm=ns, seq_tiles=1, transpose_v=False)
            kv_s = (kf, vf)
            q6 = q.reshape(ntok, nseq, kvh, GROUP, 2, HEAD_DIM).transpose(1, 2, 4, 3, 0, 5)
            q6 = q6.reshape(nseq, kvh, 2, GROUP * ntok, HEAD_DIM)
            qblk = (q6[:, :, :, :, None, None, :] * eye_h[None, :, None, None, :, None, None]
                    * eye_c[None, None, :, None, None, :, None])
            qblk = qblk.reshape(nseq, kvh * 2 * GROUP * ntok, kvw)
            pad_new = lambda a: jnp.pad(jnp.swapaxes(a.reshape(ntok, nseq, kvw), 0, 1),
                                        ((0, 0), (0, PAGE_SIZE - ntok), (0, 0)))
            od = _flash_sample(page_table, qblk, cache_k4, cache_v4, j, pad_new(kb), pad_new(vb), lams, sg,
                               ppc=ppc, ntok=ntok, lam_init=lam_init)
            o = od.reshape(nseq, kvh, GROUP, ntok, VAL_DIM).transpose(3, 0, 1, 2, 4).reshape(ns, e)
            ys = _attn_out(o.astype(BF16), gs, wout, ys, fg, tm=ns)

    y_prompt = yp.reshape(batch, seq, d)
    y_sample = jnp.swapaxes(ys.reshape(ntok, nseq, d), 0, 1)
    rows_p = lambda a: a.reshape(n_attn, batch, seq, kvh, VAL_DIM)
    rows_s = lambda a: jnp.swapaxes(a.reshape(n_attn, ntok, nseq, kvh, VAL_DIM), 1, 2)
    return (y_prompt, y_sample, jnp.stack(conv_p), jnp.stack(conv_s),
            rows_p(kv_p[0]), rows_p(kv_p[1]), rows_s(kv_s[0]), rows_s(kv_s[1]))
```

```python
import functools
import math

import jax
import jax.numpy as jnp
from jax import lax
from jax.experimental import pallas as pl
from jax.experimental.pallas import tpu as pltpu

F32 = jnp.float32
BF16 = jnp.bfloat16

N_MIXERS = 2
CONV_WIDTH = 31
HEAD_DIM = 64
VAL_DIM = 2 * HEAD_DIM
GROUP = 4
ROT_DIM = HEAD_DIM // 4
ROPE_THETA = 500000.0
RMS_EPS = 1e-6
LN_EPS = 1e-5
SCALE = HEAD_DIM ** -0.5
Q_SCALE = SCALE * math.log2(math.e)
PAGE_SIZE = 128

LANES = 128
HALO = 32
VMEM_LIMIT = 56 * 1024 * 1024


def _cparams(sem):
    return pltpu.CompilerParams(dimension_semantics=sem, vmem_limit_bytes=VMEM_LIMIT)


def _const_spec(shape, single=False):
    nd = len(shape)
    return pl.BlockSpec(shape, lambda *_: (0,) * nd, pipeline_mode=pl.Buffered(1) if single else None)


def _rms(x, g):
    return x * lax.rsqrt(jnp.mean(x * x, axis=-1, keepdims=True) + RMS_EPS) * g


def _sigmoid(x):
    return 1.0 / (1.0 + jnp.exp(-x))


def _conv_tail(cb_ref, m_ref, gate, lng_ref, lnb_ref, n_chunks):
    tot = cb_ref[0]
    for c in range(1, n_chunks):
        tot = tot + cb_ref[c]
    width = n_chunks * LANES
    mu = jnp.sum(tot, axis=-1, keepdims=True) / width
    sq = None
    for c in range(n_chunks):
        d = cb_ref[c] - mu
        sq = d * d if sq is None else sq + d * d
    rstd = lax.rsqrt(jnp.sum(sq, axis=-1, keepdims=True) / width + LN_EPS)
    for c in range(n_chunks):
        sl = slice(c * LANES, (c + 1) * LANES)
        cn = (cb_ref[c] - mu) * rstd * lng_ref[:, sl] + lnb_ref[:, sl]
        act = cn * _sigmoid(cn)
        g = gate[:, sl]
        m_ref[:, sl] = (act * (g * _sigmoid(g))).astype(BF16)


def _conv_prompt_kernel(x_ref, ng_ref, win_ref, dww_ref, dwb_ref, lng_ref, lnb_ref, wout_ref,
                        y_ref, st_ref, upad_ref, cb_ref, m_ref, *, tm, nt, e):
    t = pl.program_id(1)
    n_chunks = e // LANES

    @pl.when(t == 0)
    def _():
        upad_ref[:, 0:HALO, :] = jnp.zeros((n_chunks, HALO, LANES), F32)

    x = x_ref[...]
    h = _rms(x, ng_ref[...]).astype(BF16)
    a = jnp.dot(h, win_ref[:, 0:e], preferred_element_type=F32)
    b = jnp.dot(h, win_ref[:, e:2 * e], preferred_element_type=F32)
    u = a * _sigmoid(b)
    for c in range(n_chunks):
        upad_ref[c, HALO:HALO + tm, :] = u[:, c * LANES:(c + 1) * LANES]

    off = HALO - (CONV_WIDTH - 1)

    def chunk(c, carry):
        acc = jnp.broadcast_to(dwb_ref[c], (tm, LANES))
        for k in range(CONV_WIDTH):
            acc = acc + dww_ref[c, k:k + 1, :] * upad_ref[c, off + k:off + k + tm, :]
        cb_ref[c] = acc
        return carry

    lax.fori_loop(0, n_chunks, chunk, 0)

    gate = jnp.dot(h, win_ref[:, 2 * e:3 * e], preferred_element_type=F32)
    _conv_tail(cb_ref, m_ref, gate, lng_ref, lnb_ref, n_chunks)
    y_ref[...] = jnp.dot(m_ref[...], wout_ref[...], preferred_element_type=F32) + x

    @pl.when(t == nt - 1)
    def _():
        for c in range(n_chunks):
            st_ref[0, :, c * LANES:(c + 1) * LANES] = upad_ref[c, tm + off:tm + HALO, :]

    upad_ref[:, 0:HALO, :] = upad_ref[:, tm:tm + HALO, :]


def _conv_prompt(x, ng, win, dww3, dwb3, lng, lnb, wout, *, batch, seq, tm):
    n, d = x.shape
    e = wout.shape[0]
    nt = seq // tm
    n_chunks = e // LANES
    kern = functools.partial(_conv_prompt_kernel, tm=tm, nt=nt, e=e)
    return pl.pallas_call(
        kern,
        grid=(batch, nt),
        in_specs=[
            pl.BlockSpec((tm, d), lambda b, t: (b * nt + t, 0)),
            _const_spec((1, d)),
            _const_spec(win.shape, single=True),
            _const_spec(dww3.shape),
            _const_spec(dwb3.shape),
            _const_spec((1, e)),
            _const_spec((1, e)),
            _const_spec(wout.shape, single=True),
        ],
        out_specs=[
            pl.BlockSpec((tm, d), lambda b, t: (b * nt + t, 0)),
            pl.BlockSpec((1, CONV_WIDTH - 1, e), lambda b, t: (b, 0, 0)),
        ],
        out_shape=[
            jax.ShapeDtypeStruct((n, d), F32),
            jax.ShapeDtypeStruct((batch, CONV_WIDTH - 1, e), F32),
        ],
        scratch_shapes=[
            pltpu.VMEM((n_chunks, HALO + tm, LANES), F32),
            pltpu.VMEM((n_chunks, tm, LANES), F32),
            pltpu.VMEM((tm, e), BF16),
        ],
        compiler_params=_cparams(("arbitrary", "arbitrary")),
        name="conv_prompt",
    )(x, ng, win, dww3, dwb3, lng, lnb, wout)


def _conv_sample_kernel(x_ref, ng_ref, win_ref, prev_ref, dww_ref, dwb_ref, lng_ref, lnb_ref, wout_ref,
                        y_ref, st_ref, u_ref, cb_ref, m_ref, *, nseq, ntok, e):
    n_chunks = e // LANES
    hist = CONV_WIDTH - 1
    x = x_ref[...]
    h = _rms(x, ng_ref[...]).astype(BF16)
    a = jnp.dot(h, win_ref[:, 0:e], preferred_element_type=F32)
    b = jnp.dot(h, win_ref[:, e:2 * e], preferred_element_type=F32)
    u_ref[...] = a * _sigmoid(b)

    def row(r, sl):
        if r < hist:
            return prev_ref[r, :, sl]
        return u_ref[(r - hist) * nseq:(r - hist + 1) * nseq, sl]

    for c in range(n_chunks):
        sl = slice(c * LANES, (c + 1) * LANES)
        for t in range(ntok):
            acc = jnp.broadcast_to(dwb_ref[c], (nseq, LANES))
            for k in range(CONV_WIDTH):
                acc = acc + dww_ref[c, k:k + 1, :] * row(t + k, sl)
            cb_ref[c, t * nseq:(t + 1) * nseq, :] = acc
        for r in range(hist):
            st_ref[r, :, sl] = row(r + ntok, sl)

    gate = jnp.dot(h, win_ref[:, 2 * e:3 * e], preferred_element_type=F32)
    _conv_tail(cb_ref, m_ref, gate, lng_ref, lnb_ref, n_chunks)
    y_ref[...] = jnp.dot(m_ref[...], wout_ref[...], preferred_element_type=F32) + x


def _conv_sample(x, ng, win, prev_t, dww3, dwb3, lng, lnb, wout, *, nseq, ntok):
    n, d = x.shape
    e = wout.shape[0]
    n_chunks = e // LANES
    kern = functools.partial(_conv_sample_kernel, nseq=nseq, ntok=ntok, e=e)
    args = (x, ng, win, prev_t, dww3, dwb3, lng, lnb, wout)
    return pl.pallas_call(
        kern,
        grid=(1,),
        in_specs=[_const_spec(a.shape) for a in args],
        out_specs=[_const_spec((n, d)), _const_spec(prev_t.shape)],
        out_shape=[jax.ShapeDtypeStruct((n, d), F32), jax.ShapeDtypeStruct(prev_t.shape, F32)],
        scratch_shapes=[
            pltpu.VMEM((n, e), F32),
            pltpu.VMEM((n_chunks, n, LANES), F32),
            pltpu.VMEM((n, e), BF16),
        ],
        compiler_params=_cparams(("arbitrary",)),
        name="conv_sample",
    )(*args)


def _rope_tile(x, cos, sin_up, sin_dn):
    return (x * cos + pltpu.roll(x, LANES - ROT_DIM // 2, axis=1) * sin_up
            + pltpu.roll(x, ROT_DIM // 2, axis=1) * sin_dn)


def _attn_proj_kernel(x_ref, ng_ref, win_ref, cos_ref, sup_ref, sdn_ref, *rest, qw, kvw, transpose_v, layer):
    q_ref, kf_ref, kb_ref, vf_ref, vb_ref, gs_ref = rest[-6:]
    if layer:
        kf_ref[0:layer] = rest[0][...]
        vf_ref[0:layer] = rest[1][...]
    kvh = kvw // LANES
    tm = x_ref.shape[0]
    x = x_ref[...]
    h = _rms(x, ng_ref[...]).astype(BF16)
    cos, sup, sdn = cos_ref[...], sup_ref[...], sdn_ref[...]
    q = jnp.dot(h, win_ref[:, 0:qw], preferred_element_type=F32)
    for j in range(qw // LANES):
        sl = slice(j * LANES, (j + 1) * LANES)
        q_ref[:, sl] = (_rope_tile(q[:, sl], cos, sup, sdn) * Q_SCALE).astype(BF16)
    k = jnp.dot(h, win_ref[:, qw:qw + kvw], preferred_element_type=F32)
    for j in range(kvh):
        sl = slice(j * LANES, (j + 1) * LANES)
        kr = _rope_tile(k[:, sl], cos, sup, sdn)
        kf_ref[layer, pl.ds(j, tm, stride=kvh), :] = kr
        kb_ref[:, sl] = kr.astype(BF16)
    v = jnp.dot(h, win_ref[:, qw + kvw:qw + 2 * kvw], preferred_element_type=F32)
    for j in range(kvh):
        vf_ref[layer, pl.ds(j, tm, stride=kvh), :] = v[:, j * LANES:(j + 1) * LANES]
    if transpose_v:
        vt = v.T.astype(BF16)
        for j in range(kvh):
            vb_ref[0, 0, j] = vt[j * LANES:(j + 1) * LANES, :]
    else:
        vb_ref[...] = v.astype(BF16)
    g = jnp.dot(h, win_ref[:, qw + 2 * kvw:], preferred_element_type=F32)
    gs_ref[...] = (g * _sigmoid(g)).astype(BF16)


def _attn_proj(x, ng, win, cos, sup, sdn, kv_rows, *, layer, qw, kvw, tm, seq_tiles, transpose_v):
    n, d = x.shape
    nt = n // tm
    kvh = kvw // LANES
    gw = win.shape[1] - qw - 2 * kvw
    row = lambda w: pl.BlockSpec((tm, w), lambda i: (i, 0))
    tab = pl.BlockSpec((tm, LANES), lambda i: (i % seq_tiles, 0))
    if transpose_v:
        vb_spec = pl.BlockSpec((1, 1, kvh, LANES, tm), lambda i: (i // seq_tiles, i % seq_tiles, 0, 0, 0))
        vb_shape = jax.ShapeDtypeStruct((nt // seq_tiles, seq_tiles, kvh, LANES, tm), BF16)
    else:
        vb_spec = row(kvw)
        vb_shape = jax.ShapeDtypeStruct((n, kvw), BF16)
    assert (kv_rows is None) == (layer == 0)
    kern = functools.partial(_attn_proj_kernel, qw=qw, kvw=kvw, transpose_v=transpose_v, layer=layer)
    rows_spec = pl.BlockSpec((layer + 1, tm * kvh, LANES), lambda i: (0, i, 0))
    rows_shape = jax.ShapeDtypeStruct((layer + 1, n * kvh, LANES), F32)
    in_specs = [row(d), _const_spec((1, d)), _const_spec(win.shape), tab, tab, tab]
    args = [x, ng, win, cos, sup, sdn]
    if layer:
        in_specs += [pl.BlockSpec((layer, tm * kvh, LANES), lambda i: (0, i, 0))] * 2
        args += list(kv_rows)
    return pl.pallas_call(
        kern,
        grid=(nt,),
        in_specs=in_specs,
        out_specs=[row(qw), rows_spec, row(kvw), rows_spec, vb_spec, row(gw)],
        out_shape=[
            jax.ShapeDtypeStruct((n, qw), BF16),
            rows_shape,
            jax.ShapeDtypeStruct((n, kvw), BF16),
            rows_shape,
            vb_shape,
            jax.ShapeDtypeStruct((n, gw), BF16),
        ],
        compiler_params=_cparams(("arbitrary",)),
        name="attn_proj",
    )(*args)


def _diff_lambda(lq1_ref, lk1_ref, lq2_ref, lk2_ref, lam_init):
    s1 = jnp.sum(lq1_ref[...] * lk1_ref[...], axis=-1, keepdims=True)
    s2 = jnp.sum(lq2_ref[...] * lk2_ref[...], axis=-1, keepdims=True)
    return jnp.exp(s1) - jnp.exp(s2) + lam_init


def _flash_prompt_kernel(q_ref, qn_ref, k_ref, vt_ref, tri_ref, lq1_ref, lk1_ref, lq2_ref, lk2_ref, sg_ref, o_ref,
                         qp_ref, m_ref, l_ref, acc_ref, s0_ref, s1_ref, s2_ref, x0_ref, x1_ref, x2_ref,
                         *, tq, sb, ct, lam_init):
    qi = pl.program_id(2)
    r = 2 * GROUP * tq
    n_ct = r // ct
    nt_dims = (((1,), (1,)), ((), ()))

    def pad_queries(src_ref):
        lane = lax.broadcasted_iota(jnp.int32, (tq, LANES), 1)
        zero = jnp.zeros((tq, LANES), BF16)
        for g in range(GROUP):
            qg = src_ref[:, g * LANES:(g + 1) * LANES]
            qp_ref[(2 * g) * tq:(2 * g + 1) * tq, :] = jnp.where(lane < HEAD_DIM, qg, zero)
            qp_ref[(2 * g + 1) * tq:(2 * g + 2) * tq, :] = jnp.where(lane >= HEAD_DIM, qg, zero)

    def step(nxt, cur, nxt_diag=False):
        if nxt is not None:
            sn_ref, xn_ref, jn = nxt
        if cur is not None:
            sc_ref, xc_ref, jc, diag = cur
        for c in range(n_ct):
            cs = slice(c * ct, (c + 1) * ct)
            t0 = (c * ct) % tq
            if nxt is not None:
                rows = t0 + ct if nxt_diag else sb
                kb = k_ref[pl.ds(pl.multiple_of(jn * sb, sb), rows), :]
                sn = lax.dot_general(kb, qp_ref[cs, :], nt_dims, preferred_element_type=F32)
                sn_ref[0:rows, cs] = sn
                if not nxt_diag:
                    xn_ref[:, cs] = jnp.max(sn, axis=0, keepdims=True)
            if cur is None:
                continue
            if diag:
                rows = t0 + ct
                s = sc_ref[t0:rows, cs] + tri_ref[...]
                if t0:
                    s = jnp.concatenate([sc_ref[0:t0, cs], s], axis=0)
                mx = jnp.max(s, axis=0, keepdims=True)
            else:
                rows = sb
                s = sc_ref[:, cs]
                mx = xc_ref[:, cs]
            vt = vt_ref[0, jc, 0, :, 0:rows]
            m_old = m_ref[:, cs]
            m_new = jnp.maximum(m_old, mx)
            alpha = jnp.exp2(m_old - m_new)
            p = jnp.exp2(s - m_new)
            l_ref[:, cs] = alpha * l_ref[:, cs] + jnp.sum(p, axis=0, keepdims=True)
            pv = jnp.dot(vt, p.astype(BF16), preferred_element_type=F32)
            acc_ref[:, cs] = alpha * acc_ref[:, cs] + pv
            m_ref[:, cs] = m_new

    b0 = (s0_ref, x0_ref)
    b1 = (s1_ref, x1_ref)
    b2 = (s2_ref, x2_ref)

    @pl.when(qi == 0)
    def _():
        pad_queries(q_ref)
        step(b2 + (0,), None)

    m_ref[...] = jnp.full((1, r), -jnp.inf, F32)
    l_ref[...] = jnp.zeros((1, r), F32)
    acc_ref[...] = jnp.zeros((VAL_DIM, r), F32)

    nv = (qi * tq) // sb

    @pl.when(nv >= 1)
    def _():
        step(b1 + (1,), b2 + (0, False))

    rest = jnp.maximum(nv - 1, 0)

    def pair(jj, carry):
        j0 = 1 + 2 * jj
        step(b0 + (j0 + 1,), b1 + (j0, False))
        step(b1 + (j0 + 2,), b0 + (j0 + 1, False))
        return carry

    lax.fori_loop(0, rest // 2, pair, 0)
    rest_odd = lax.rem(rest, 2) == 1

    @pl.when(nv == 0)
    def _():
        step(None, b2 + (0, True))
        pad_queries(qn_ref)
        step(b2 + (0,), None)

    @pl.when(jnp.logical_and(nv >= 1, jnp.logical_not(rest_odd)))
    def _():
        pad_queries(qn_ref)
        step(b2 + (0,), b1 + (nv, True))

    @pl.when(jnp.logical_and(nv >= 1, rest_odd))
    def _():
        step(b0 + (nv,), b1 + (nv - 1, False), nxt_diag=True)
        pad_queries(qn_ref)
        step(b2 + (0,), b0 + (nv, True))

    lam = _diff_lambda(lq1_ref, lk1_ref, lq2_ref, lk2_ref, lam_init)
    for g in range(GROUP):
        s1 = slice((2 * g) * tq, (2 * g + 1) * tq)
        s2 = slice((2 * g + 1) * tq, (2 * g + 2) * tq)
        o1 = acc_ref[:, s1] * (1.0 / l_ref[:, s1])
        o2 = acc_ref[:, s2] * (1.0 / l_ref[:, s2])
        d = o1 - lam * o2
        dn = d * lax.rsqrt(jnp.mean(d * d, axis=0, keepdims=True) + RMS_EPS)
        o_ref[:, g * LANES:(g + 1) * LANES] = ((dn.T * sg_ref[...]) * (1.0 - lam_init)).astype(BF16)


def _flash_prompt(q, kb, vt5, lams, sg, *, batch, seq, tq, ct, lam_init):
    n, qw = q.shape
    kvh = kb.shape[1] // LANES
    sb = vt5.shape[-1]
    nq = seq // tq
    r = 2 * GROUP * tq
    assert tq == sb and tq % ct == 0, (tq, sb, ct)
    kern = functools.partial(_flash_prompt_kernel, tq=tq, sb=sb, ct=ct, lam_init=lam_init)
    vec = _const_spec((1, HEAD_DIM))
    tri = jnp.where(lax.broadcasted_iota(jnp.int32, (ct, ct), 0) <= lax.broadcasted_iota(jnp.int32, (ct, ct), 1),
                    0.0, -jnp.inf).astype(F32)
    return pl.pallas_call(
        kern,
        grid=(batch, kvh, nq),
        in_specs=[
            pl.BlockSpec((tq, GROUP * LANES), lambda b, h, i: (b * nq + i, h)),
            pl.BlockSpec((tq, GROUP * LANES), lambda b, h, i: (b * nq + jnp.minimum(i + 1, nq - 1), h)),
            pl.BlockSpec((seq, LANES), lambda b, h, i: (b, h)),
            pl.BlockSpec((1, seq // sb, 1, LANES, sb), lambda b, h, i: (b, 0, h, 0, 0)),
            _const_spec((ct, ct)),
            vec, vec, vec, vec,
            _const_spec((1, VAL_DIM)),
        ],
        out_specs=pl.BlockSpec((tq, GROUP * LANES), lambda b, h, i: (b * nq + i, h)),
        out_shape=jax.ShapeDtypeStruct((n, qw), BF16),
        scratch_shapes=[
            pltpu.VMEM((r, LANES), BF16),
            pltpu.VMEM((1, r), F32),
            pltpu.VMEM((1, r), F32),
            pltpu.VMEM((VAL_DIM, r), F32),
            pltpu.VMEM((sb, r), F32),
            pltpu.VMEM((sb, r), F32),
            pltpu.VMEM((sb, r), F32),
            pltpu.VMEM((1, r), F32),
            pltpu.VMEM((1, r), F32),
            pltpu.VMEM((1, r), F32),
        ],
        compiler_params=_cparams(("arbitrary", "arbitrary", "arbitrary")),
        name="flash_prompt",
    )(q, q, kb, vt5, tri, *lams, sg)


def _flash_sample_kernel(pt_ref, q_ref, *refs, ppc, ntok, lam_init):
    del pt_ref
    k_refs = refs[0:ppc]
    v_refs = refs[ppc:2 * ppc]
    kn_ref, vn_ref, lq1_ref, lk1_ref, lq2_ref, lk2_ref, sg_ref, o_ref, m_ref, l_ref, acc_ref = refs[2 * ppc:]
    c = pl.program_id(1)
    rows = q_ref.shape[1]
    kvh = q_ref.shape[2] // LANES
    rpk = rows // kvh

    @pl.when(c == 0)
    def _():
        m_ref[...] = jnp.full(m_ref.shape, -jnp.inf, F32)
        l_ref[...] = jnp.zeros(l_ref.shape, F32)
        acc_ref[...] = jnp.zeros(acc_ref.shape, F32)

    q = q_ref[0]

    def update(s_list, v_list):
        s = jnp.concatenate(s_list, axis=1) if len(s_list) > 1 else s_list[0]
        m_old = m_ref[...]
        m_new = jnp.maximum(m_old, jnp.max(s, axis=1, keepdims=True))
        alpha = jnp.exp2(m_old - m_new)
        p = jnp.exp2(s - m_new)
        l_ref[...] = alpha * l_ref[...] + jnp.sum(p, axis=1, keepdims=True)
        pb = p.astype(BF16)
        pv = None
        for i, v in enumerate(v_list):
            d = jnp.dot(pb[:, i * PAGE_SIZE:(i + 1) * PAGE_SIZE], v, preferred_element_type=F32)
            pv = d if pv is None else pv + d
        acc_ref[...] = alpha * acc_ref[...] + pv
        m_ref[...] = m_new

    def page(ref):
        heads = [ref[0, 0, pl.ds(h, PAGE_SIZE, stride=kvh), :] for h in range(kvh)]
        return jnp.concatenate(heads, axis=1).astype(BF16)

    nt_dims = (((1,), (1,)), ((), ()))
    s_list = [lax.dot_general(q, page(k_refs[i]), nt_dims, preferred_element_type=F32) for i in range(ppc)]
    update(s_list, [page(v_refs[i]) for i in range(ppc)])

    @pl.when(c == pl.num_programs(1) - 1)
    def _():
        s = lax.dot_general(q, kn_ref[0], nt_dims, preferred_element_type=F32)
        key = lax.broadcasted_iota(jnp.int32, s.shape, 1)
        tok = lax.rem(lax.broadcasted_iota(jnp.int32, s.shape, 0), ntok)
        update([jnp.where(key <= tok, s, -jnp.inf)], [vn_ref[0]])
        lam = _diff_lambda(lq1_ref, lk1_ref, lq2_ref, lk2_ref, lam_init)
        half = rpk // 2
        for h in range(kvh):
            blk = (acc_ref[h * rpk:(h + 1) * rpk, h * LANES:(h + 1) * LANES]
                   / l_ref[h * rpk:(h + 1) * rpk, :])
            d = blk[0:half] - lam * blk[half:rpk]
            dn = d * lax.rsqrt(jnp.mean(d * d, axis=-1, keepdims=True) + RMS_EPS)
            o_ref[0, h * half:(h + 1) * half, :] = (dn * sg_ref[...]) * (1.0 - lam_init)


def _flash_sample(page_table, qblk, cache_k4, cache_v4, layer, knew, vnew, lams, sg, *, ppc, ntok, lam_init):
    nseq, rows, kw = qblk.shape
    n_pages = page_table.shape[1]
    nch = n_pages // ppc
    kvh = kw // LANES
    kern = functools.partial(_flash_sample_kernel, ppc=ppc, ntok=ntok, lam_init=lam_init)

    def page_spec(i):
        return pl.BlockSpec((1, 1, PAGE_SIZE * kvh, LANES), lambda n, c, pt: (layer, pt[n, c * ppc + i], 0, 0))

    seq_spec = pl.BlockSpec((1, PAGE_SIZE, kw), lambda n, c, pt: (n, 0, 0))
    vec = pl.BlockSpec((1, HEAD_DIM), lambda n, c, pt: (0, 0))
    out_rows = rows // 2
    grid_spec = pltpu.PrefetchScalarGridSpec(
        num_scalar_prefetch=1,
        grid=(nseq, nch),
        in_specs=([pl.BlockSpec((1, rows, kw), lambda n, c, pt: (n, 0, 0))]
                  + [page_spec(i) for i in range(ppc)] + [page_spec(i) for i in range(ppc)]
                  + [seq_spec, seq_spec, vec, vec, vec, vec,
                     pl.BlockSpec((1, VAL_DIM), lambda n, c, pt: (0, 0))]),
        out_specs=pl.BlockSpec((1, out_rows, VAL_DIM), lambda n, c, pt: (n, 0, 0)),
        scratch_shapes=[
            pltpu.VMEM((rows, 1), F32),
            pltpu.VMEM((rows, 1), F32),
            pltpu.VMEM((rows, kw), F32),
        ],
    )
    return pl.pallas_call(
        kern,
        grid_spec=grid_spec,
        out_shape=jax.ShapeDtypeStruct((nseq, out_rows, VAL_DIM), F32),
        compiler_params=_cparams(("arbitrary", "arbitrary")),
        name="flash_sample",
    )(page_table, qblk, *([cache_k4] * ppc), *([cache_v4] * ppc), knew, vnew, *lams, sg)


def _attn_out_kernel(o_ref, gs_ref, w_ref, x_ref, *rest, final):
    if final:
        fg_ref, y_ref = rest
    else:
        (y_ref,) = rest
    y = jnp.dot(o_ref[...] * gs_ref[...], w_ref[...], preferred_element_type=F32) + x_ref[...]
    if final:
        y = _rms(y, fg_ref[...])
    y_ref[...] = y


def _attn_out(o, gs, w, x, fg, *, tm):
    n, d = x.shape
    e = w.shape[0]
    final = fg is not None
    row = lambda wd: pl.BlockSpec((tm, wd), lambda i: (i, 0))
    in_specs = [row(e), row(e), _const_spec(w.shape), row(d)]
    args = [o, gs, w, x]
    if final:
        in_specs.append(_const_spec((1, d)))
        args.append(fg)
    return pl.pallas_call(
        functools.partial(_attn_out_kernel, final=final),
        grid=(n // tm,),
        in_specs=in_specs,
        out_specs=row(d),
        out_shape=jax.ShapeDtypeStruct((n, d), F32),
        compiler_params=_cparams(("arbitrary",)),
        name="attn_out",
    )(*args)


def _final_norm_kernel(x_ref, g_ref, y_ref):
    y_ref[...] = _rms(x_ref[...], g_ref[...])


def _final_norm(x, g, *, tm):
    n, d = x.shape
    row = pl.BlockSpec((tm, d), lambda i: (i, 0))
    return pl.pallas_call(
        _final_norm_kernel, grid=(n // tm,), in_specs=[row, _const_spec((1, d))], out_specs=row,
        out_shape=jax.ShapeDtypeStruct((n, d), F32), compiler_params=_cparams(("arbitrary",)),
        name="final_norm",
    )(x, g)


def _rope_tables(pos):
    half = ROT_DIM // 2
    inv_freq = ROPE_THETA ** (-jnp.arange(half, dtype=F32) / half)
    ang = pos.astype(F32)[:, None] * inv_freq[None, :]
    cos, sin = jnp.cos(ang), jnp.sin(ang)
    n = pos.shape[0]
    pad = jnp.zeros((n, HEAD_DIM - ROT_DIM), F32)
    zero = jnp.zeros((n, half), F32)
    c64 = jnp.concatenate([cos, cos, pad + 1.0], axis=1)
    up64 = jnp.concatenate([-sin, zero, pad], axis=1)
    dn64 = jnp.concatenate([zero, sin, pad], axis=1)
    rep = lambda t: jnp.concatenate([t, t], axis=1)
    return rep(c64), rep(up64), rep(dn64)


def _pick_tile(n, pref):
    t = min(n, pref)
    while n % t:
        t //= 2
    return t


def kernel(x_prompt, x_sample, state_conv, cache_k, cache_v, page_table, norm_g, final_norm_g,
           conv_w_in, conv_dw_w, conv_dw_b, conv_ln_g, conv_ln_b, conv_w_out,
           attn_w_in, attn_lambda_q1, attn_lambda_k1, attn_lambda_q2, attn_lambda_k2,
           attn_subln_g, attn_w_out):
    batch, seq, d = x_prompt.shape
    nseq, ntok, _ = x_sample.shape
    depth = norm_g.shape[0]
    e = conv_w_out.shape[1]
    n_heads = e // VAL_DIM
    kvh = n_heads // GROUP
    qw = n_heads * 2 * HEAD_DIM
    kvw = kvh * 2 * HEAD_DIM
    n_chunks = e // LANES
    n_pages = page_table.shape[1]
    past = n_pages * PAGE_SIZE

    tm_conv = _pick_tile(seq, 512)
    tm_proj = _pick_tile(seq, 512)
    tm_out = _pick_tile(batch * seq, 1024)
    tq = _pick_tile(seq, 512)
    ct = 256
    ppc = _pick_tile(n_pages, 32)

    yp = x_prompt.reshape(batch * seq, d)
    ys = jnp.swapaxes(x_sample, 0, 1).reshape(ntok * nseq, d)
    ns = ntok * nseq

    tabs_p = _rope_tables(jnp.arange(seq, dtype=jnp.int32))
    tabs_s = _rope_tables(past + jnp.repeat(jnp.arange(ntok, dtype=jnp.int32), nseq))

    cache_k4 = cache_k.reshape(cache_k.shape[0], cache_k.shape[1], PAGE_SIZE * kvh, 2 * HEAD_DIM)
    cache_v4 = cache_v.reshape(cache_v.shape[0], cache_v.shape[1], PAGE_SIZE * kvh, VAL_DIM)
    eye_h = jnp.eye(kvh, dtype=BF16)
    eye_c = jnp.eye(2, dtype=BF16)

    n_attn = depth // N_MIXERS
    conv_p, conv_s = [], []
    kv_p = kv_s = None
    for i in range(depth):
        j = i // N_MIXERS
        ng = norm_g[i].reshape(1, d)
        last = i == depth - 1
        if i % N_MIXERS == 0:
            win = conv_w_in[j].astype(BF16)
            wout = conv_w_out[j].astype(BF16)
            dww3 = jnp.pad(conv_dw_w[j], ((0, HALO - CONV_WIDTH), (0, 0))).reshape(HALO, n_chunks, LANES)
            dww3 = jnp.swapaxes(dww3, 0, 1)
            dwb3 = conv_dw_b[j].reshape(n_chunks, 1, LANES)
            lng = conv_ln_g[j].reshape(1, e)
            lnb = conv_ln_b[j].reshape(1, e)
            yp, sp = _conv_prompt(yp, ng, win, dww3, dwb3, lng, lnb, wout, batch=batch, seq=seq, tm=tm_conv)
            prev_t = jnp.swapaxes(state_conv[j], 0, 1)
            ys, ss_t = _conv_sample(ys, ng, win, prev_t, dww3, dwb3, lng, lnb, wout, nseq=nseq, ntok=ntok)
            conv_p.append(sp)
            conv_s.append(jnp.swapaxes(ss_t, 0, 1))
            if last:
                yp = _final_norm(yp, final_norm_g.reshape(1, d), tm=tm_proj)
                ys = _final_norm(ys, final_norm_g.reshape(1, d), tm=ns)
        else:
            lam_init = 0.8 - 0.6 * math.exp(-0.3 * i)
            win = attn_w_in[j].astype(BF16)
            wout = attn_w_out[j].astype(BF16)
            lams = [a[j].reshape(1, HEAD_DIM) for a in
                    (attn_lambda_q1, attn_lambda_k1, attn_lambda_q2, attn_lambda_k2)]
            sg = attn_subln_g[j].reshape(1, VAL_DIM)
            fg = final_norm_g.reshape(1, d) if last else None

            q, kf, kb, vf, vt5, gs = _attn_proj(yp, ng, win, *tabs_p, kv_p, layer=j, qw=qw,
                                                kvw=kvw, tm=tm_proj, seq_tiles=seq // tm_proj, transpose_v=True)
            kv_p = (kf, vf)
            o = _flash_prompt(q, kb, vt5, lams, sg, batch=batch, seq=seq, tq=tq, ct=ct, lam_init=lam_init)
            yp = _attn_out(o, gs, wout, yp, fg, tm=tm_out)

            q, kf, kb, vf, vb, gs = _attn_proj(ys, ng, win, *tabs_s, kv_s, layer=j, qw=qw,
                                               kvw=kvw, tm=ns, seq_tiles=1, transpose_v=False)
            kv_s = (kf, vf)
            q6 = q.reshape(ntok, nseq, kvh, GROUP, 2, HEAD_DIM).transpose(1, 2, 4, 3, 0, 5)
            q6 = q6.reshape(nseq, kvh, 2, GROUP * ntok, HEAD_DIM)
            qblk = (q6[:, :, :, :, None, None, :] * eye_h[None, :, None, None, :, None, None]
                    * eye_c[None, None, :, None, None, :, None])
            qblk = qblk.reshape(nseq, kvh * 2 * GROUP * ntok, kvw)
            pad_new = lambda a: jnp.pad(jnp.swapaxes(a.reshape(ntok, nseq, kvw), 0, 1),
                                        ((0, 0), (0, PAGE_SIZE - ntok), (0, 0)))
            od = _flash_sample(page_table, qblk, cache_k4, cache_v4, j, pad_new(kb), pad_new(vb), lams, sg,
                               ppc=ppc, ntok=ntok, lam_init=lam_init)
            o = od.reshape(nseq, kvh, GROUP, ntok, VAL_DIM).transpose(3, 0, 1, 2, 4).reshape(ns, e)
            ys = _attn_out(o.astype(BF16), gs, wout, ys, fg, tm=ns)

    y_prompt = yp.reshape(batch, seq, d)
    y_sample = jnp.swapaxes(ys.reshape(ntok, nseq, d), 0, 1)
    rows_p = lambda a: a.reshape(n_attn, batch, seq, kvh, VAL_DIM)
    rows_s = lambda a: jnp.swapaxes(a.reshape(n_attn, ntok, nseq, kvh, VAL_DIM), 1, 2)
    return (y_prompt, y_sample, jnp.stack(conv_p), jnp.stack(conv_s),
            rows_p(kv_p[0]), rows_p(kv_p[1]), rows_s(kv_s[0]), rows_s(kv_s[1]))
```
